```python
import math
import jax, jax.numpy as jnp
from jax import lax
import numpy as np

D_MODEL = 1024
BATCH = 4
SEQ = 8192
DEPTH = 1

N_Q_HEADS = 8
N_KV_HEADS = 4
GQA_GROUP = N_Q_HEADS // N_KV_HEADS
HEAD_DIM = 128
WINDOW = 128
BLOCK = 128
N_HALO = -(-WINDOW // BLOCK)
NEG_INF = -1e30
CONV_CH = D_MODEL
CONV_WIDTH = 31
CONV_PAD = (CONV_WIDTH - 1) // 2
CONV_IN_W = 2 * CONV_CH
Q_W = N_Q_HEADS * HEAD_DIM
KV_W = N_KV_HEADS * HEAD_DIM
GATE_W = 2 * D_MODEL
Q_OFF = CONV_IN_W
K_OFF = Q_OFF + Q_W
V_OFF = K_OFF + KV_W
G_OFF = V_OFF + KV_W
IN_WIDTH = G_OFF + GATE_W
N_GROUPS = 4
EXPERTS_PER_GROUP = 8
N_EXPERTS = N_GROUPS * EXPERTS_PER_GROUP
TOP_K = 2
EXPERT_HIDDEN = D_MODEL // 4
LN_EPS = 1e-5
DEEPNORM_ALPHA = (2.0 * DEPTH) ** 0.25
DEEPNORM_BETA = (8.0 * DEPTH) ** -0.25

kernel_name = "hybrid_conv_swa_hmoe_encoder"


def layer_norm(x, g, b):
    xf = x.astype(jnp.float32)
    mu = xf.mean(-1, keepdims=True)
    var = jnp.square(xf - mu).mean(-1, keepdims=True)
    return ((xf - mu) * lax.rsqrt(var + LN_EPS) * g + b).astype(x.dtype)


def conformer_conv(u, w_dw, b_dw, ln_g, ln_b, w_proj):
    a, gate = jnp.split(u, 2, axis=-1)
    h = a * jax.nn.sigmoid(gate)
    h = lax.conv_general_dilated(
        h, w_dw[:, None, :].astype(h.dtype), window_strides=(1,),
        padding=((CONV_PAD, CONV_PAD),),
        dimension_numbers=("NWC", "WIO", "NWC"),
        feature_group_count=CONV_CH) + b_dw
    h = jax.nn.silu(layer_norm(h, ln_g, ln_b))
    return h @ w_proj


def windowed_gqa(q, k, v, sink, slopes):
    B, S = q.shape[0], q.shape[1]
    nb = S // BLOCK
    span = (2 * N_HALO + 1) * BLOCK
    qb = q.reshape(B, nb, BLOCK, N_KV_HEADS, GQA_GROUP, HEAD_DIM)

    def band(t):
        tp = jnp.pad(t, ((0, 0), (N_HALO * BLOCK, N_HALO * BLOCK), (0, 0), (0, 0)))
        tp = tp.reshape(B, nb + 2 * N_HALO, BLOCK, N_KV_HEADS, HEAD_DIM)
        return jnp.concatenate([tp[:, o:o + nb] for o in range(2 * N_HALO + 1)], axis=2)

    kb, vb = band(k), band(v)
    s = jnp.einsum("bnqhgd,bnshd->bnhgqs", qb, kb,
                   preferred_element_type=jnp.float32) * (HEAD_DIM ** -0.5)
    qi = jnp.arange(BLOCK)[:, None]
    kj = jnp.arange(span)[None, :]
    rel = kj - N_HALO * BLOCK - qi
    kpos = jnp.arange(nb)[:, None, None] * BLOCK - N_HALO * BLOCK + kj
    valid = (jnp.abs(rel) <= WINDOW)[None] & (kpos >= 0) & (kpos < S)
    dist = jnp.abs(rel).astype(jnp.float32)
    alibi = -slopes.reshape(N_KV_HEADS, GQA_GROUP)[:, :, None, None] * dist
    s = jnp.where(valid[None, :, None, None], s + alibi, NEG_INF)
    sk = sink.astype(jnp.float32).reshape(N_KV_HEADS, GQA_GROUP)[:, :, None, None]
    m = jnp.maximum(s.max(-1, keepdims=True), sk)
    p = jnp.exp(s - m)
    p = p / (p.sum(-1, keepdims=True) + jnp.exp(sk - m))
    o = jnp.einsum("bnhgqs,bnshd->bnqhgd", p.astype(v.dtype), vb)
    return o.reshape(B, S, Q_W).astype(q.dtype)


def hierarchical_moe(x, w_rg, b_rg, w_re, b_re, w_gate_up, w_down):
    B, S, D = x.shape
    t = x.reshape(B * S, D)
    g_logits = (t @ w_rg + b_rg).astype(jnp.float32)
    g_prob = jax.nn.softmax(g_logits, axis=-1)
    g_idx = jnp.argmax(g_logits, axis=-1)
    g_w = jnp.take_along_axis(g_prob, g_idx[:, None], axis=-1)
    e_logits = (t @ w_re + b_re).astype(jnp.float32).reshape(-1, N_GROUPS, EXPERTS_PER_GROUP)
    e_in = jnp.take_along_axis(e_logits, g_idx[:, None, None], axis=1)[:, 0]
    e_prob = jax.nn.softmax(e_in, axis=-1)
    top_w, top_i = lax.top_k(e_prob, TOP_K)
    top_w = top_w / top_w.sum(-1, keepdims=True)
    expert_id = g_idx[:, None] * EXPERTS_PER_GROUP + top_i
    weight = g_w * top_w
    comb = (jax.nn.one_hot(expert_id, N_EXPERTS, dtype=jnp.float32) * weight[..., None]).sum(1)
    out = jnp.zeros((B * S, D), jnp.float32)
    for e in range(N_EXPERTS):
        hg, hu = jnp.split(t @ w_gate_up[e], 2, axis=-1)
        out = out + comb[:, e:e + 1] * ((jax.nn.silu(hg) * hu) @ w_down[e])
    return out.astype(x.dtype).reshape(B, S, D)


def setup_inputs(seed: int = 0) -> dict:
    key = jax.random.key(seed)
    ks = jax.random.split(key, 24)
    L, D = DEPTH, D_MODEL
    f32 = jnp.float32

    def nrm(k, shape, scale):
        return jax.random.normal(k, shape, f32) * scale

    x = nrm(ks[0], (BATCH, SEQ, D), 1.0)
    w_in = nrm(ks[1], (L, D, IN_WIDTH), D ** -0.5)
    w_in = w_in.at[:, :, V_OFF:V_OFF + KV_W].multiply(DEEPNORM_BETA)
    return {
        "x": x,
        "w_in": w_in,
        "b_gate": nrm(ks[2], (L, GATE_W), 0.02),
        "w_dw": nrm(ks[3], (L, CONV_WIDTH, CONV_CH), CONV_WIDTH ** -0.5),
        "b_dw": nrm(ks[4], (L, CONV_CH), 0.02),
        "conv_ln_g": 1.0 + nrm(ks[5], (L, CONV_CH), 0.02),
        "conv_ln_b": nrm(ks[6], (L, CONV_CH), 0.02),
        "w_conv_proj": nrm(ks[7], (L, CONV_CH, D), CONV_CH ** -0.5),
        "attn_sink": nrm(ks[8], (L, N_Q_HEADS), 0.5),
        "w_attn_proj": nrm(ks[9], (L, Q_W, D), Q_W ** -0.5),
        "w_out": nrm(ks[10], (L, D, D), D ** -0.5 * DEEPNORM_BETA),
        "ln1_g": 1.0 + nrm(ks[11], (L, D), 0.02),
        "ln1_b": nrm(ks[12], (L, D), 0.02),
        "w_router_group": nrm(ks[13], (L, D, N_GROUPS), D ** -0.5),
        "b_router_group": nrm(ks[14], (L, N_GROUPS), 0.01),
        "w_router_expert": nrm(ks[15], (L, D, N_EXPERTS), D ** -0.5),
        "b_router_expert": nrm(ks[16], (L, N_EXPERTS), 0.01),
        "w_gate_up": nrm(ks[17], (L, N_EXPERTS, D, 2 * EXPERT_HIDDEN), D ** -0.5),
        "w_down": nrm(ks[18], (L, N_EXPERTS, EXPERT_HIDDEN, D), EXPERT_HIDDEN ** -0.5 * DEEPNORM_BETA),
        "ln2_g": 1.0 + nrm(ks[19], (L, D), 0.02),
        "ln2_b": nrm(ks[20], (L, D), 0.02),
    }


def reference(x, w_in, b_gate, w_dw, b_dw, conv_ln_g, conv_ln_b, w_conv_proj, attn_sink,
              w_attn_proj, w_out, ln1_g, ln1_b, w_router_group, b_router_group,
              w_router_expert, b_router_expert, w_gate_up, w_down, ln2_g, ln2_b):
    B, S, _ = x.shape
    slopes = jnp.exp2(-8.0 * jnp.arange(1, N_Q_HEADS + 1, dtype=jnp.float32) / N_Q_HEADS)
    for l in range(DEPTH):
        proj = x @ w_in[l]
        conv_u = proj[..., :Q_OFF]
        q = proj[..., Q_OFF:K_OFF].reshape(B, S, N_Q_HEADS, HEAD_DIM)
        k = proj[..., K_OFF:V_OFF].reshape(B, S, N_KV_HEADS, HEAD_DIM)
        v = proj[..., V_OFF:G_OFF].reshape(B, S, N_KV_HEADS, HEAD_DIM)
        gates = jax.nn.sigmoid(proj[..., G_OFF:] + b_gate[l])
        g_conv, g_attn = gates[..., :D_MODEL], gates[..., D_MODEL:]
        conv_out = conformer_conv(conv_u, w_dw[l], b_dw[l], conv_ln_g[l], conv_ln_b[l], w_conv_proj[l])
        attn_out = windowed_gqa(q, k, v, attn_sink[l], slopes) @ w_attn_proj[l]
        mixed = (g_conv * conv_out + g_attn * attn_out) @ w_out[l]
        x = layer_norm(DEEPNORM_ALPHA * x + mixed, ln1_g[l], ln1_b[l])
        moe = hierarchical_moe(x, w_router_group[l], b_router_group[l], w_router_expert[l],
                               b_router_expert[l], w_gate_up[l], w_down[l])
        x = layer_norm(DEEPNORM_ALPHA * x + moe, ln2_g[l], ln2_b[l])
    return x
```

```python
import functools

import jax
import jax.numpy as jnp
from jax import lax
from jax.experimental import pallas as pl
from jax.experimental.pallas import tpu as pltpu

D_MODEL = 1024
N_Q_HEADS = 8
N_KV_HEADS = 4
HEAD_DIM = 128
WINDOW = 128
NEG_INF = -1e30
CONV_WIDTH = 31
CONV_PAD = (CONV_WIDTH - 1) // 2
Q_W = N_Q_HEADS * HEAD_DIM
KV_W = N_KV_HEADS * HEAD_DIM
Q_OFF = 2 * D_MODEL
K_OFF = Q_OFF + Q_W
V_OFF = K_OFF + KV_W
G_OFF = V_OFF + KV_W
IN_WIDTH = G_OFF + 2 * D_MODEL
N_GROUPS = 4
EXPERTS_PER_GROUP = 8
N_EXPERTS = N_GROUPS * EXPERTS_PER_GROUP
EXPERT_HIDDEN = D_MODEL // 4
LN_EPS = 1e-5
DEPTH = 1
DEEPNORM_ALPHA = (2.0 * DEPTH) ** 0.25

LANES = 128
SUBLANES = 8
VMEM_LIMIT_BYTES = 56 * 1024 * 1024

TILE_A = 512
TILE_B = 256
CONV_HALO = 16
TILE_R = 512
TILE_P = 2048
TILE_M = 256
TILE_F = 256
N_BUCKETS = N_GROUPS * EXPERTS_PER_GROUP * EXPERTS_PER_GROUP
N_PAIRS = N_GROUPS * (EXPERTS_PER_GROUP * (EXPERTS_PER_GROUP - 1) // 2)
AUG_W = D_MODEL + 2 * LANES

_NT = (((1,), (1,)), ((), ()))


def _dot(a, b):
    return jnp.dot(a, b, preferred_element_type=jnp.float32)


def _dot_nt(a, b):
    return lax.dot_general(a, b, _NT, preferred_element_type=jnp.float32)


def _sigmoid(x):
    return 1.0 / (1.0 + jnp.exp(-x))


def _layer_norm(y, g, b):
    mu = jnp.mean(y, axis=-1, keepdims=True)
    d = y - mu
    var = jnp.mean(d * d, axis=-1, keepdims=True)
    return d * lax.rsqrt(var + LN_EPS) * g + b


def _split_bf16(x):
    hi = x.astype(jnp.bfloat16)
    lo = (x - hi.astype(jnp.float32)).astype(jnp.bfloat16)
    return hi, lo


def _inproj_kernel(x_ref, w_ref, bg_ref, h_ref, q_ref, k_ref, v_ref, gt_ref):
    xb = x_ref[...].astype(jnp.bfloat16)
    a = _dot(xb, w_ref[:, 0:D_MODEL])
    g = _dot(xb, w_ref[:, D_MODEL:Q_OFF])
    h_ref[...] = a * _sigmoid(g)
    q = _dot(xb, w_ref[:, Q_OFF:K_OFF]) * (HEAD_DIM ** -0.5)
    q_ref[...] = q.astype(jnp.bfloat16)
    k_ref[...] = _dot(xb, w_ref[:, K_OFF:V_OFF]).astype(jnp.bfloat16)
    v_ref[...] = _dot(xb, w_ref[:, V_OFF:G_OFF]).astype(jnp.bfloat16)
    gt = _dot(xb, w_ref[:, G_OFF:IN_WIDTH]) + bg_ref[...]
    gt_ref[...] = _sigmoid(gt)


def _inproj(x2, w_in_bf, b_gate):
    n = x2.shape[0]
    const = lambda i: (0, 0)
    row = lambda i: (i, 0)
    return pl.pallas_call(
        _inproj_kernel,
        grid=(n // TILE_A,),
        in_specs=[
            pl.BlockSpec((TILE_A, D_MODEL), row),
            pl.BlockSpec((D_MODEL, IN_WIDTH), const, pipeline_mode=pl.Buffered(1)),
            pl.BlockSpec((1, 2 * D_MODEL), const),
        ],
        out_specs=[
            pl.BlockSpec((TILE_A, D_MODEL), row),
            pl.BlockSpec((TILE_A, Q_W), row),
            pl.BlockSpec((TILE_A, KV_W), row),
            pl.BlockSpec((TILE_A, KV_W), row),
            pl.BlockSpec((TILE_A, 2 * D_MODEL), row),
        ],
        out_shape=[
            jax.ShapeDtypeStruct((n, D_MODEL), jnp.float32),
            jax.ShapeDtypeStruct((n, Q_W), jnp.bfloat16),
            jax.ShapeDtypeStruct((n, KV_W), jnp.bfloat16),
            jax.ShapeDtypeStruct((n, KV_W), jnp.bfloat16),
            jax.ShapeDtypeStruct((n, 2 * D_MODEL), jnp.float32),
        ],
        compiler_params=pltpu.CompilerParams(
            dimension_semantics=("arbitrary",), vmem_limit_bytes=VMEM_LIMIT_BYTES),
        name="inproj",
    )(x2, w_in_bf, b_gate)


def _conv_block(hs_ref, wdw_ref, row0, lane0, rows):
    ng = rows // SUBLANES
    nw = ng + 2 * CONV_HALO // SUBLANES
    lanes = slice(lane0, lane0 + LANES)
    win = [hs_ref[row0 + SUBLANES * g: row0 + SUBLANES * (g + 1), lanes] for g in range(nw)]
    sub = lax.broadcasted_iota(jnp.int32, (SUBLANES, LANES), 0)
    acc = [jnp.zeros((SUBLANES, LANES), jnp.float32) for _ in range(ng)]
    for s in range(1, SUBLANES + 1):
        if s == SUBLANES:
            shifted = win[1:]
        else:
            rolled = [pltpu.roll(w, SUBLANES - s, axis=0) for w in win]
            keep = sub < (SUBLANES - s)
            shifted = [jnp.where(keep, rolled[g], rolled[g + 1]) for g in range(nw - 1)]
        for a in range((CONV_WIDTH + SUBLANES - 1) // SUBLANES):
            j = s - 1 + SUBLANES * a
            if j >= CONV_WIDTH:
                continue
            wv = wdw_ref[SUBLANES * j: SUBLANES * (j + 1), lanes]
            for g in range(ng):
                acc[g] = acc[g] + wv * shifted[g + a]
    return acc


def _mix_kernel(hl_ref, hc_ref, hr_ref, q_ref, kl_ref, kc_ref, kr_ref, vl_ref, vc_ref, vr_ref,
                gt_ref, x_ref, wdw_ref, bdw_ref, cg_ref, cb_ref, wcp_ref, bias_ref, sink_ref,
                wap_ref, wo_ref, g1_ref, b1_ref, x1_ref, hs_ref, cv_ref, at_ref):
    i = pl.program_id(1)
    first = i == 0
    last = i == pl.num_programs(1) - 1
    tb = TILE_B

    hs_ref[0:CONV_HALO, :] = jnp.where(first, 0.0, hl_ref[...])
    hs_ref[CONV_HALO:CONV_HALO + tb, :] = hc_ref[...]
    hs_ref[CONV_HALO + tb:, :] = jnp.where(last, 0.0, hr_ref[...])
    rows = 128
    for r0 in range(0, tb, rows):
        for lc in range(D_MODEL // LANES):
            acc = _conv_block(hs_ref, wdw_ref, r0, lc * LANES, rows)
            for g, a in enumerate(acc):
                cv_ref[r0 + SUBLANES * g: r0 + SUBLANES * (g + 1), lc * LANES:(lc + 1) * LANES] = a
    c = _layer_norm(cv_ref[...] + bdw_ref[...], cg_ref[...], cb_ref[...])
    c = (c * _sigmoid(c)).astype(jnp.bfloat16)
    conv_out = _dot(c, wcp_ref[...])

    nq = tb // 128
    neg_l = jnp.where(first, NEG_INF, 0.0)
    neg_r = jnp.where(last, NEG_INF, 0.0)
    lane = lax.broadcasted_iota(jnp.int32, (1, 3 * 128), 1)
    edge_l = jnp.where(lane < 128, neg_l, 0.0)
    edge_r = jnp.where(lane >= 256, neg_r, 0.0)
    for j in range(nq):
        rs = slice(j * 128, (j + 1) * 128)
        for g in range(N_KV_HEADS):
            cs = slice(g * HEAD_DIM, (g + 1) * HEAD_DIM)
            if j == 0:
                k0, v0 = kl_ref[:, cs], vl_ref[:, cs]
            else:
                k0, v0 = kc_ref[(j - 1) * 128:j * 128, cs], vc_ref[(j - 1) * 128:j * 128, cs]
            if j == nq - 1:
                k2, v2 = kr_ref[:, cs], vr_ref[:, cs]
            else:
                k2, v2 = kc_ref[(j + 1) * 128:(j + 2) * 128, cs], vc_ref[(j + 1) * 128:(j + 2) * 128, cs]
            k3 = jnp.concatenate([k0, kc_ref[rs, cs], k2], axis=0)
            v3 = jnp.concatenate([v0, vc_ref[rs, cs], v2], axis=0)
            h0 = 2 * g
            q2 = jnp.concatenate([q_ref[rs, h0 * HEAD_DIM:(h0 + 1) * HEAD_DIM],
                                  q_ref[rs, (h0 + 1) * HEAD_DIM:(h0 + 2) * HEAD_DIM]], axis=0)
            s = _dot_nt(q2, k3) + bias_ref[g]
            if j == 0:
                s = s + edge_l
            if j == nq - 1:
                s = s + edge_r
            sk = sink_ref[g]
            m = jnp.maximum(jnp.max(s, axis=1, keepdims=True), sk)
            p = jnp.exp(s - m)
            den = jnp.sum(p, axis=1, keepdims=True) + jnp.exp(sk - m)
            o = _dot(p.astype(jnp.bfloat16), v3) / den
            at_ref[rs, h0 * HEAD_DIM:(h0 + 1) * HEAD_DIM] = o[0:128].astype(jnp.bfloat16)
            at_ref[rs, (h0 + 1) * HEAD_DIM:(h0 + 2) * HEAD_DIM] = o[128:256].astype(jnp.bfloat16)
    attn_out = _dot(at_ref[...], wap_ref[...])

    merged = gt_ref[:, 0:D_MODEL] * conv_out + gt_ref[:, D_MODEL:] * attn_out
    mixed = _dot(merged.astype(jnp.bfloat16), wo_ref[...])
    x1_ref[...] = _layer_norm(DEEPNORM_ALPHA * x_ref[...] + mixed, g1_ref[...], b1_ref[...])


def _mix(h3, q3, k3, v3, gt3, x3, wdw8, b_dw, cg, cb, wcp, bias_tab, sink_tab, wap, wo, g1, b1):
    nb, s, _ = x3.shape
    tb = TILE_B
    nt = s // tb
    hb = tb // CONV_HALO
    kb = tb // 128
    const2 = lambda b, i: (0, 0)
    const3 = lambda b, i: (0, 0, 0)
    cur = lambda b, i: (b, i, 0)
    w_spec = lambda shape: pl.BlockSpec(shape, const2, pipeline_mode=pl.Buffered(1))
    in_specs = [
        pl.BlockSpec((None, CONV_HALO, D_MODEL), lambda b, i: (b, jnp.maximum(i * hb - 1, 0), 0)),
        pl.BlockSpec((None, tb, D_MODEL), cur),
        pl.BlockSpec((None, CONV_HALO, D_MODEL),
                     lambda b, i: (b, jnp.minimum((i + 1) * hb, s // CONV_HALO - 1), 0)),
        pl.BlockSpec((None, tb, Q_W), cur),
    ]
    for _ in range(2):
        in_specs += [
            pl.BlockSpec((None, 128, KV_W), lambda b, i: (b, jnp.maximum(i * kb - 1, 0), 0)),
            pl.BlockSpec((None, tb, KV_W), cur),
            pl.BlockSpec((None, 128, KV_W), lambda b, i: (b, jnp.minimum((i + 1) * kb, s // 128 - 1), 0)),
        ]
    in_specs += [
        pl.BlockSpec((None, tb, 2 * D_MODEL), cur),
        pl.BlockSpec((None, tb, D_MODEL), cur),
        w_spec((CONV_WIDTH * SUBLANES, D_MODEL)),
        pl.BlockSpec((1, D_MODEL), const2),
        pl.BlockSpec((1, D_MODEL), const2),
        pl.BlockSpec((1, D_MODEL), const2),
        w_spec((D_MODEL, D_MODEL)),
        pl.BlockSpec((N_KV_HEADS, 256, 3 * 128), const3, pipeline_mode=pl.Buffered(1)),
        pl.BlockSpec((N_KV_HEADS, 256, 1), const3),
        w_spec((Q_W, D_MODEL)),
        w_spec((D_MODEL, D_MODEL)),
        pl.BlockSpec((1, D_MODEL), const2),
        pl.BlockSpec((1, D_MODEL), const2),
    ]
    return pl.pallas_call(
        _mix_kernel,
        grid=(nb, nt),
        in_specs=in_specs,
        out_specs=pl.BlockSpec((None, tb, D_MODEL), cur),
        out_shape=jax.ShapeDtypeStruct((nb, s, D_MODEL), jnp.float32),
        scratch_shapes=[
            pltpu.VMEM((tb + 2 * CONV_HALO, D_MODEL), jnp.float32),
            pltpu.VMEM((tb, D_MODEL), jnp.float32),
            pltpu.VMEM((tb, Q_W), jnp.bfloat16),
        ],
        compiler_params=pltpu.CompilerParams(
            dimension_semantics=("arbitrary", "arbitrary"), vmem_limit_bytes=VMEM_LIMIT_BYTES),
        name="mix",
    )(h3, h3, h3, q3, k3, k3, k3, v3, v3, v3, gt3, x3, wdw8, b_dw, cg, cb, wcp, bias_tab,
      sink_tab, wap, wo, g1, b1)


def _route_kernel(x_ref, wr_ref, br_ref, u_ref, aug_ref, bkt_ref, rank_ref, cnt_ref, carry_ref):
    i = pl.program_id(0)
    tr = TILE_R

    @pl.when(i == 0)
    def _():
        carry_ref[...] = jnp.zeros_like(carry_ref)

    x = x_ref[...]
    xh, xl = _split_bf16(x)
    wh, wl = _split_bf16(wr_ref[...])
    lt = _dot_nt(wh, xh) + _dot_nt(wh, xl) + _dot_nt(wl, xh) + br_ref[...]
    le = lt[0:N_EXPERTS]
    lg = lt[N_EXPERTS:N_EXPERTS + SUBLANES]
    iota8 = lax.broadcasted_iota(jnp.int32, (SUBLANES, tr), 0)

    gmax = jnp.max(lg, axis=0, keepdims=True)
    gidx = jnp.min(jnp.where(lg == gmax, iota8, SUBLANES), axis=0, keepdims=True)
    gw = 1.0 / jnp.sum(jnp.exp(lg - gmax), axis=0, keepdims=True)

    ein = le[0:EXPERTS_PER_GROUP]
    for g in range(1, N_GROUPS):
        ein = jnp.where(gidx == g, le[g * EXPERTS_PER_GROUP:(g + 1) * EXPERTS_PER_GROUP], ein)
    ee = jnp.exp(ein - jnp.max(ein, axis=0, keepdims=True))
    prob = ee / jnp.sum(ee, axis=0, keepdims=True)
    p1 = jnp.max(prob, axis=0, keepdims=True)
    i1 = jnp.min(jnp.where(prob == p1, iota8, SUBLANES), axis=0, keepdims=True)
    rest = jnp.where(iota8 == i1, -1.0, prob)
    p2 = jnp.max(rest, axis=0, keepdims=True)
    i2 = jnp.min(jnp.where(rest == p2, iota8, SUBLANES), axis=0, keepdims=True)
    den = p1 + p2
    w1 = gw * (p1 / den)
    w2 = gw * (p2 / den)
    first_lo = i1 < i2
    lo = jnp.minimum(i1, i2)
    hi = jnp.maximum(i1, i2)
    w_lo = jnp.where(first_lo, w1, w2)
    w_hi = jnp.where(first_lo, w2, w1)
    bkt = gidx * (EXPERTS_PER_GROUP * EXPERTS_PER_GROUP) + lo * EXPERTS_PER_GROUP + hi
    bkt_ref[...] = bkt

    onehot = lax.broadcasted_iota(jnp.int32, (N_BUCKETS, tr), 0) == bkt
    ob = jnp.where(onehot, 1.0, 0.0).astype(jnp.bfloat16)
    before = _dot(ob, u_ref[...])
    within = jnp.sum(jnp.where(onehot, before, 0.0), axis=0, keepdims=True)
    carry = carry_ref[...]
    c_hi = jnp.floor(carry * (1.0 / 256.0))
    c_lo = carry - 256.0 * c_hi
    prev = 256.0 * _dot(c_hi.astype(jnp.bfloat16), ob) + _dot(c_lo.astype(jnp.bfloat16), ob)
    rank_ref[...] = (within + prev[0:1]).astype(jnp.int32)
    carry_ref[...] = carry + _dot_nt(jnp.ones((SUBLANES, tr), jnp.bfloat16), ob)

    @pl.when(i == pl.num_programs(0) - 1)
    def _():
        cnt_ref[...] = carry_ref[...]

    aug_ref[:, 0:D_MODEL] = x
    aug_ref[:, D_MODEL:D_MODEL + LANES] = jnp.broadcast_to(w_lo, (LANES, tr)).T
    aug_ref[:, D_MODEL + LANES:] = jnp.broadcast_to(w_hi, (LANES, tr)).T


def _route(x1, wr, br, upper):
    n = x1.shape[0]
    tr = TILE_R
    const = lambda i: (0, 0)
    return pl.pallas_call(
        _route_kernel,
        grid=(n // tr,),
        in_specs=[
            pl.BlockSpec((tr, D_MODEL), lambda i: (i, 0)),
            pl.BlockSpec(wr.shape, const),
            pl.BlockSpec(br.shape, const),
            pl.BlockSpec((tr, tr), const),
        ],
        out_specs=[
            pl.BlockSpec((tr, AUG_W), lambda i: (i, 0)),
            pl.BlockSpec((1, tr), lambda i: (0, i)),
            pl.BlockSpec((1, tr), lambda i: (0, i)),
            pl.BlockSpec((SUBLANES, N_BUCKETS), const),
        ],
        out_shape=[
            jax.ShapeDtypeStruct((n, AUG_W), jnp.float32),
            jax.ShapeDtypeStruct((1, n), jnp.int32),
            jax.ShapeDtypeStruct((1, n), jnp.int32),
            jax.ShapeDtypeStruct((SUBLANES, N_BUCKETS), jnp.float32),
        ],
        scratch_shapes=[pltpu.VMEM((SUBLANES, N_BUCKETS), jnp.float32)],
        compiler_params=pltpu.CompilerParams(
            dimension_semantics=("arbitrary",), vmem_limit_bytes=VMEM_LIMIT_BYTES),
        name="route",
    )(x1, wr, br, upper)


def _plan_kernel(bkt_ref, rank_ref, cnt_ref, u_ref, pos_ref, tb_ref, tv_ref, *, n_tiles_pad):
    tm = TILE_M
    cnt = cnt_ref[...]
    ntl = jnp.floor((cnt + (tm - 1)) * (1.0 / tm))
    tstart = _dot(ntl.astype(jnp.bfloat16), u_ref[...])
    tend = tstart + ntl

    def pick(table, onehot_bf):
        t_hi = jnp.floor(table * (1.0 / 256.0))
        t_lo = table - 256.0 * t_hi
        return (256.0 * _dot(t_hi.astype(jnp.bfloat16), onehot_bf)
                + _dot(t_lo.astype(jnp.bfloat16), onehot_bf))[0:1]

    bkt = bkt_ref[...]
    onehot = lax.broadcasted_iota(jnp.int32, (N_BUCKETS, bkt.shape[1]), 0) == bkt
    ob = jnp.where(onehot, 1.0, 0.0).astype(jnp.bfloat16)
    pos_ref[...] = pick(tstart, ob).astype(jnp.int32) * tm + rank_ref[...]

    eye = (lax.broadcasted_iota(jnp.int32, (N_BUCKETS, N_BUCKETS), 0)
           == lax.broadcasted_iota(jnp.int32, (N_BUCKETS, N_BUCKETS), 1))
    tend_col = jnp.sum(jnp.where(eye, tend[0:1], 0.0), axis=1, keepdims=True)
    tile = lax.broadcasted_iota(jnp.int32, (1, n_tiles_pad), 1).astype(jnp.float32)
    tbk = jnp.sum(jnp.where(tend_col <= tile, 1.0, 0.0), axis=0, keepdims=True)
    tbk_i = tbk.astype(jnp.int32)
    oh_t = lax.broadcasted_iota(jnp.int32, (N_BUCKETS, n_tiles_pad), 0) == tbk_i
    oh_tb = jnp.where(oh_t, 1.0, 0.0).astype(jnp.bfloat16)
    left = pick(cnt, oh_tb) - (tile - pick(tstart, oh_tb)) * tm
    tv_ref[...] = jnp.clip(left, 0.0, float(tm)).astype(jnp.int32)
    tb_ref[...] = jnp.minimum(tbk_i, N_BUCKETS - 1)


def _plan(bkt, rank, cnt, upper, n_tiles_pad):
    n = bkt.shape[1]
    const = lambda i: (0, 0)
    return pl.pallas_call(
        functools.partial(_plan_kernel, n_tiles_pad=n_tiles_pad),
        grid=(n // TILE_P,),
        in_specs=[
            pl.BlockSpec((1, TILE_P), lambda i: (0, i)),
            pl.BlockSpec((1, TILE_P), lambda i: (0, i)),
            pl.BlockSpec((SUBLANES, N_BUCKETS), const),
            pl.BlockSpec((N_BUCKETS, N_BUCKETS), const),
        ],
        out_specs=[
            pl.BlockSpec((1, TILE_P), lambda i: (0, i)),
            pl.BlockSpec((1, n_tiles_pad), const),
            pl.BlockSpec((1, n_tiles_pad), const),
        ],
        out_shape=[
            jax.ShapeDtypeStruct((1, n), jnp.int32),
            jax.ShapeDtypeStruct((1, n_tiles_pad), jnp.int32),
            jax.ShapeDtypeStruct((1, n_tiles_pad), jnp.int32),
        ],
        compiler_params=pltpu.CompilerParams(dimension_semantics=("arbitrary",)),
        name="plan",
    )(bkt, rank, cnt, upper)


def _row_copy(aug_hbm, xbuf, sem, slot, tok, r):
    return pltpu.make_async_copy(aug_hbm.at[pl.ds(tok, 1), :], xbuf.at[slot, pl.ds(r, 1), :],
                                 sem.at[slot])


def _moe_kernel(pos_ref, tb_ref, tv_ref, aug_hbm, wgu1_ref, wgu2_ref, wd1_ref, wd2_ref, g2_ref,
                b2_ref, y_ref, src_ref, xbuf, sem, *, n_tokens, n_tiles):
    i = pl.program_id(0)
    tm = TILE_M
    slot = i % 2

    def start_gather(tile, slot_):
        base = tile * tm
        for r in range(tm):
            _row_copy(aug_hbm, xbuf, sem, slot_, src_ref[base + r], r).start()

    @pl.when(i == 0)
    def _():
        def zero(c, _):
            for u in range(8):
                src_ref[c * 8 + u] = 0
            return 0
        lax.fori_loop(0, (n_tiles * tm) // 8, zero, 0)

        def scatter(c, _):
            for u in range(8):
                t = c * 8 + u
                src_ref[pos_ref[t]] = t
            return 0
        lax.fori_loop(0, n_tokens // 8, scatter, 0)

        @pl.when(tv_ref[0] > 0)
        def _():
            start_gather(0, 0)

    used = tv_ref[i] > 0

    @pl.when(used)
    def _():
        for r in range(tm):
            _row_copy(aug_hbm, xbuf, sem, slot, 0, r).wait()

        @pl.when(jnp.logical_and(i + 1 < n_tiles, tv_ref[jnp.minimum(i + 1, n_tiles - 1)] > 0))
        def _():
            start_gather(i + 1, 1 - slot)

        xa = xbuf[slot]
        x = xa[:, 0:D_MODEL]
        xb = x.astype(jnp.bfloat16)
        acc = jnp.zeros((tm, D_MODEL), jnp.float32)
        for e, (wgu_ref, wd_ref) in enumerate(((wgu1_ref, wd1_ref), (wgu2_ref, wd2_ref))):
            gu = _dot(xb, wgu_ref[...])
            hg = gu[:, 0:EXPERT_HIDDEN]
            hu = gu[:, EXPERT_HIDDEN:]
            hid = (hg * _sigmoid(hg) * hu).astype(jnp.bfloat16)
            y = _dot(hid, wd_ref[...])
            w = xa[:, D_MODEL + e * LANES: D_MODEL + (e + 1) * LANES]
            acc = acc + jnp.concatenate([w] * (D_MODEL // LANES), axis=1) * y
        y_ref[...] = _layer_norm(DEEPNORM_ALPHA * x + acc, g2_ref[...], b2_ref[...])

    @pl.when(jnp.logical_not(used))
    def _():
        y_ref[...] = jnp.zeros_like(y_ref)


def _moe(pos, tbk, tval, aug, wgu_bf, wd_bf, g2, b2, n_tiles):
    n = pos.shape[0]
    tm = TILE_M
    epg = EXPERTS_PER_GROUP
    e_lo = lambda i, pos, tb, tv: (tb[i] // epg, 0, 0)
    e_hi = lambda i, pos, tb, tv: ((tb[i] // (epg * epg)) * epg + tb[i] % epg, 0, 0)
    const = lambda i, pos, tb, tv: (0, 0)
    grid_spec = pltpu.PrefetchScalarGridSpec(
        num_scalar_prefetch=3,
        grid=(n_tiles,),
        in_specs=[
            pl.BlockSpec(memory_space=pl.ANY),
            pl.BlockSpec((None, D_MODEL, 2 * EXPERT_HIDDEN), e_lo),
            pl.BlockSpec((None, D_MODEL, 2 * EXPERT_HIDDEN), e_hi),
            pl.BlockSpec((None, EXPERT_HIDDEN, D_MODEL), e_lo),
            pl.BlockSpec((None, EXPERT_HIDDEN, D_MODEL), e_hi),
            pl.BlockSpec((1, D_MODEL), const),
            pl.BlockSpec((1, D_MODEL), const),
        ],
        out_specs=pl.BlockSpec((tm, D_MODEL), lambda i, pos, tb, tv: (i, 0)),
        scratch_shapes=[
            pltpu.SMEM((n_tiles * tm,), jnp.int32),
            pltpu.VMEM((2, tm, AUG_W), jnp.float32),
            pltpu.SemaphoreType.DMA((2,)),
        ],
    )
    return pl.pallas_call(
        functools.partial(_moe_kernel, n_tokens=n, n_tiles=n_tiles),
        grid_spec=grid_spec,
        out_shape=jax.ShapeDtypeStruct((n_tiles * tm, D_MODEL), jnp.float32),
        compiler_params=pltpu.CompilerParams(
            dimension_semantics=("arbitrary",), vmem_limit_bytes=VMEM_LIMIT_BYTES),
        name="moe",
    )(pos, tbk, tval, aug, wgu_bf, wgu_bf, wd_bf, wd_bf, g2, b2)


def _unsort_kernel(pos_ref, y_hbm, o_ref, sem):
    i = pl.program_id(0)
    tf = TILE_F
    copies = [pltpu.make_async_copy(y_hbm.at[pl.ds(pos_ref[i * tf + r], 1), :],
                                    o_ref.at[pl.ds(r, 1), :], sem.at[0]) for r in range(tf)]
    for c in copies:
        c.start()
    for c in copies:
        c.wait()


def _unsort(pos, ys, n):
    grid_spec = pltpu.PrefetchScalarGridSpec(
        num_scalar_prefetch=1,
        grid=(n // TILE_F,),
        in_specs=[pl.BlockSpec(memory_space=pl.ANY)],
        out_specs=pl.BlockSpec((TILE_F, D_MODEL), lambda i, pos: (i, 0)),
        scratch_shapes=[pltpu.SemaphoreType.DMA((1,))],
    )
    return pl.pallas_call(
        _unsort_kernel,
        grid_spec=grid_spec,
        out_shape=jax.ShapeDtypeStruct((n, D_MODEL), jnp.float32),
        compiler_params=pltpu.CompilerParams(dimension_semantics=("arbitrary",)),
        name="unsort",
    )(pos, ys)


def _attention_tables(attn_sink):
    slopes = jnp.exp2(-8.0 * jnp.arange(1, N_Q_HEADS + 1, dtype=jnp.float32) / N_Q_HEADS)
    qi = jnp.arange(128)[:, None]
    kj = jnp.arange(3 * 128)[None, :]
    rel = kj - 128 - qi
    dist = jnp.abs(rel).astype(jnp.float32)
    band = jnp.abs(rel) <= WINDOW
    bias = jnp.where(band[None], -slopes[:, None, None] * dist[None], NEG_INF)
    bias = bias.reshape(N_KV_HEADS, 2 * 128, 3 * 128)
    sink = jnp.broadcast_to(attn_sink.astype(jnp.float32)[:, None], (N_Q_HEADS, 128))
    return bias, sink.reshape(N_KV_HEADS, 2 * 128, 1)


def kernel(x, w_in, b_gate, w_dw, b_dw, conv_ln_g, conv_ln_b, w_conv_proj, attn_sink, w_attn_proj,
           w_out, ln1_g, ln1_b, w_router_group, b_router_group, w_router_expert, b_router_expert,
           w_gate_up, w_down, ln2_g, ln2_b):
    nb, s, d = x.shape
    n = nb * s
    assert d == D_MODEL and s % TILE_B == 0 and n % TILE_A == 0 and n % TILE_R == 0
    assert n % TILE_P == 0 and n % TILE_F == 0 and n % TILE_M == 0
    bf = jnp.bfloat16
    l = 0
    row = lambda a: a[l].reshape(1, -1)

    h, q, k, v, gt = _inproj(x.reshape(n, d), w_in[l].astype(bf), row(b_gate))

    wdw8 = jnp.repeat(w_dw[l], SUBLANES, axis=0)
    bias_tab, sink_tab = _attention_tables(attn_sink[l])
    r3 = lambda a: a.reshape(nb, s, a.shape[-1])
    x1 = _mix(r3(h), r3(q), r3(k), r3(v), r3(gt), x, wdw8, row(b_dw), row(conv_ln_g),
              row(conv_ln_b), w_conv_proj[l].astype(bf), bias_tab, sink_tab,
              w_attn_proj[l].astype(bf), w_out[l].astype(bf), row(ln1_g), row(ln1_b))

    pad = SUBLANES - N_GROUPS
    wr = jnp.concatenate([w_router_expert[l].T, w_router_group[l].T,
                          jnp.zeros((pad, d), jnp.float32)], axis=0)
    br = jnp.concatenate([b_router_expert[l], b_router_group[l],
                          jnp.full((pad,), NEG_INF, jnp.float32)]).reshape(-1, 1)
    upper_r = jnp.triu(jnp.ones((TILE_R, TILE_R), bf), k=1)
    aug, bkt, rank, cnt = _route(x1.reshape(n, d), wr, br, upper_r)

    n_tiles = n // TILE_M + N_PAIRS
    n_tiles_pad = -(-n_tiles // LANES) * LANES
    upper_b = jnp.triu(jnp.ones((N_BUCKETS, N_BUCKETS), bf), k=1)
    pos, tbk, tval = _plan(bkt, rank, cnt, upper_b, n_tiles_pad)

    ys = _moe(pos.reshape(n), tbk.reshape(-1), tval.reshape(-1), aug, w_gate_up[l].astype(bf),
              w_down[l].astype(bf), row(ln2_g), row(ln2_b), n_tiles)
    out = _unsort(pos.reshape(n), ys, n)
    return out.reshape(nb, s, d)
```

```python
import functools

import jax
import jax.numpy as jnp
from jax import lax
from jax.experimental import pallas as pl
from jax.experimental.pallas import tpu as pltpu

D_MODEL = 1024
N_Q_HEADS = 8
N_KV_HEADS = 4
HEAD_DIM = 128
WINDOW = 128
NEG_INF = -1e30
CONV_WIDTH = 31
CONV_PAD = (CONV_WIDTH - 1) // 2
Q_W = N_Q_HEADS * HEAD_DIM
KV_W = N_KV_HEADS * HEAD_DIM
Q_OFF = 2 * D_MODEL
K_OFF = Q_OFF + Q_W
V_OFF = K_OFF + KV_W
G_OFF = V_OFF + KV_W
IN_WIDTH = G_OFF + 2 * D_MODEL
N_GROUPS = 4
EXPERTS_PER_GROUP = 8
N_EXPERTS = N_GROUPS * EXPERTS_PER_GROUP
EXPERT_HIDDEN = D_MODEL // 4
LN_EPS = 1e-5
DEPTH = 1
DEEPNORM_ALPHA = (2.0 * DEPTH) ** 0.25

LANES = 128
SUBLANES = 8
VMEM_LIMIT_BYTES = 56 * 1024 * 1024

TILE_A = 512
TILE_B = 256
CONV_HALO = 16
TILE_R = 512
TILE_P = 2048
TILE_M = 256
N_BUCKETS = N_GROUPS * EXPERTS_PER_GROUP * EXPERTS_PER_GROUP
N_PAIRS = N_GROUPS * (EXPERTS_PER_GROUP * (EXPERTS_PER_GROUP - 1) // 2)
ROW_CHUNKS = D_MODEL // LANES
AUG_ROWS = ROW_CHUNKS + 2
OUT_PITCH = ROW_CHUNKS + 1

_NT = (((1,), (1,)), ((), ()))


def _dot(a, b):
    return jnp.dot(a, b, preferred_element_type=jnp.float32)


def _dot_nt(a, b):
    return lax.dot_general(a, b, _NT, preferred_element_type=jnp.float32)


def _sigmoid(x):
    return 1.0 / (1.0 + jnp.exp(-x))


def _layer_norm(y, g, b):
    mu = jnp.mean(y, axis=-1, keepdims=True)
    d = y - mu
    var = jnp.mean(d * d, axis=-1, keepdims=True)
    return d * lax.rsqrt(var + LN_EPS) * g + b


def _split_bf16(x):
    hi = x.astype(jnp.bfloat16)
    lo = (x - hi.astype(jnp.float32)).astype(jnp.bfloat16)
    return hi, lo


def _inproj_kernel(x_ref, w_ref, bg_ref, h_ref, q_ref, k_ref, v_ref, gt_ref):
    xb = x_ref[...].astype(jnp.bfloat16)
    a = _dot(xb, w_ref[:, 0:D_MODEL])
    g = _dot(xb, w_ref[:, D_MODEL:Q_OFF])
    h_ref[...] = a * _sigmoid(g)
    q = _dot(xb, w_ref[:, Q_OFF:K_OFF]) * (HEAD_DIM ** -0.5)
    q_ref[...] = q.astype(jnp.bfloat16)
    k_ref[...] = _dot(xb, w_ref[:, K_OFF:V_OFF]).astype(jnp.bfloat16)
    v_ref[...] = _dot(xb, w_ref[:, V_OFF:G_OFF]).astype(jnp.bfloat16)
    gt = _dot(xb, w_ref[:, G_OFF:IN_WIDTH]) + bg_ref[...]
    gt_ref[...] = _sigmoid(gt)


def _inproj(x2, w_in_bf, b_gate):
    n = x2.shape[0]
    const = lambda i: (0, 0)
    row = lambda i: (i, 0)
    return pl.pallas_call(
        _inproj_kernel,
        grid=(n // TILE_A,),
        in_specs=[
            pl.BlockSpec((TILE_A, D_MODEL), row),
            pl.BlockSpec((D_MODEL, IN_WIDTH), const, pipeline_mode=pl.Buffered(1)),
            pl.BlockSpec((1, 2 * D_MODEL), const),
        ],
        out_specs=[
            pl.BlockSpec((TILE_A, D_MODEL), row),
            pl.BlockSpec((TILE_A, Q_W), row),
            pl.BlockSpec((TILE_A, KV_W), row),
            pl.BlockSpec((TILE_A, KV_W), row),
            pl.BlockSpec((TILE_A, 2 * D_MODEL), row),
        ],
        out_shape=[
            jax.ShapeDtypeStruct((n, D_MODEL), jnp.float32),
            jax.ShapeDtypeStruct((n, Q_W), jnp.bfloat16),
            jax.ShapeDtypeStruct((n, KV_W), jnp.bfloat16),
            jax.ShapeDtypeStruct((n, KV_W), jnp.bfloat16),
            jax.ShapeDtypeStruct((n, 2 * D_MODEL), jnp.float32),
        ],
        compiler_params=pltpu.CompilerParams(
            dimension_semantics=("arbitrary",), vmem_limit_bytes=VMEM_LIMIT_BYTES),
        name="inproj",
    )(x2, w_in_bf, b_gate)


def _conv_block(hs_ref, wdw_ref, row0, lane0, rows):
    ng = rows // SUBLANES
    nw = ng + 2 * CONV_HALO // SUBLANES
    lanes = slice(lane0, lane0 + LANES)
    win = [hs_ref[row0 + SUBLANES * g: row0 + SUBLANES * (g + 1), lanes] for g in range(nw)]
    sub = lax.broadcasted_iota(jnp.int32, (SUBLANES, LANES), 0)
    acc = [jnp.zeros((SUBLANES, LANES), jnp.float32) for _ in range(ng)]
    for s in range(1, SUBLANES + 1):
        if s == SUBLANES:
            shifted = win[1:]
        else:
            rolled = [pltpu.roll(w, SUBLANES - s, axis=0) for w in win]
            keep = sub < (SUBLANES - s)
            shifted = [jnp.where(keep, rolled[g], rolled[g + 1]) for g in range(nw - 1)]
        for a in range((CONV_WIDTH + SUBLANES - 1) // SUBLANES):
            j = s - 1 + SUBLANES * a
            if j >= CONV_WIDTH:
                continue
            wv = wdw_ref[SUBLANES * j: SUBLANES * (j + 1), lanes]
            for g in range(ng):
                acc[g] = acc[g] + wv * shifted[g + a]
    return acc


def _mix_kernel(hl_ref, hc_ref, hr_ref, q_ref, kl_ref, kc_ref, kr_ref, vl_ref, vc_ref, vr_ref,
                gt_ref, x_ref, wdw_ref, bdw_ref, cg_ref, cb_ref, wcp_ref, bias_ref, sink_ref,
                wap_ref, wo_ref, g1_ref, b1_ref, x1_ref, hs_ref, cv_ref, at_ref):
    i = pl.program_id(1)
    first = i == 0
    last = i == pl.num_programs(1) - 1
    tb = TILE_B

    hs_ref[0:CONV_HALO, :] = jnp.where(first, 0.0, hl_ref[...])
    hs_ref[CONV_HALO:CONV_HALO + tb, :] = hc_ref[...]
    hs_ref[CONV_HALO + tb:, :] = jnp.where(last, 0.0, hr_ref[...])
    rows = 128
    for r0 in range(0, tb, rows):
        for lc in range(D_MODEL // LANES):
            acc = _conv_block(hs_ref, wdw_ref, r0, lc * LANES, rows)
            for g, a in enumerate(acc):
                cv_ref[r0 + SUBLANES * g: r0 + SUBLANES * (g + 1), lc * LANES:(lc + 1) * LANES] = a
    c = _layer_norm(cv_ref[...] + bdw_ref[...], cg_ref[...], cb_ref[...])
    c = (c * _sigmoid(c)).astype(jnp.bfloat16)
    conv_out = _dot(c, wcp_ref[...])

    nq = tb // 128
    neg_l = jnp.where(first, NEG_INF, 0.0)
    neg_r = jnp.where(last, NEG_INF, 0.0)
    lane = lax.broadcasted_iota(jnp.int32, (1, 3 * 128), 1)
    edge_l = jnp.where(lane < 128, neg_l, 0.0)
    edge_r = jnp.where(lane >= 256, neg_r, 0.0)
    for j in range(nq):
        rs = slice(j * 128, (j + 1) * 128)
        for g in range(N_KV_HEADS):
            cs = slice(g * HEAD_DIM, (g + 1) * HEAD_DIM)
            if j == 0:
                k0, v0 = kl_ref[:, cs], vl_ref[:, cs]
            else:
                k0, v0 = kc_ref[(j - 1) * 128:j * 128, cs], vc_ref[(j - 1) * 128:j * 128, cs]
            if j == nq - 1:
                k2, v2 = kr_ref[:, cs], vr_ref[:, cs]
            else:
                k2, v2 = kc_ref[(j + 1) * 128:(j + 2) * 128, cs], vc_ref[(j + 1) * 128:(j + 2) * 128, cs]
            k3 = jnp.concatenate([k0, kc_ref[rs, cs], k2], axis=0)
            v3 = jnp.concatenate([v0, vc_ref[rs, cs], v2], axis=0)
            h0 = 2 * g
            q2 = jnp.concatenate([q_ref[rs, h0 * HEAD_DIM:(h0 + 1) * HEAD_DIM],
                                  q_ref[rs, (h0 + 1) * HEAD_DIM:(h0 + 2) * HEAD_DIM]], axis=0)
            s = _dot_nt(q2, k3) + bias_ref[g]
            if j == 0:
                s = s + edge_l
            if j == nq - 1:
                s = s + edge_r
            sk = sink_ref[g]
            m = jnp.maximum(jnp.max(s, axis=1, keepdims=True), sk)
            p = jnp.exp(s - m)
            den = jnp.sum(p, axis=1, keepdims=True) + jnp.exp(sk - m)
            o = _dot(p.astype(jnp.bfloat16), v3) / den
            at_ref[rs, h0 * HEAD_DIM:(h0 + 1) * HEAD_DIM] = o[0:128].astype(jnp.bfloat16)
            at_ref[rs, (h0 + 1) * HEAD_DIM:(h0 + 2) * HEAD_DIM] = o[128:256].astype(jnp.bfloat16)
    attn_out = _dot(at_ref[...], wap_ref[...])

    merged = gt_ref[:, 0:D_MODEL] * conv_out + gt_ref[:, D_MODEL:] * attn_out
    mixed = _dot(merged.astype(jnp.bfloat16), wo_ref[...])
    x1_ref[...] = _layer_norm(DEEPNORM_ALPHA * x_ref[...] + mixed, g1_ref[...], b1_ref[...])


def _mix(h3, q3, k3, v3, gt3, x3, wdw8, b_dw, cg, cb, wcp, bias_tab, sink_tab, wap, wo, g1, b1):
    nb, s, _ = x3.shape
    tb = TILE_B
    nt = s // tb
    hb = tb // CONV_HALO
    kb = tb // 128
    const2 = lambda b, i: (0, 0)
    const3 = lambda b, i: (0, 0, 0)
    cur = lambda b, i: (b, i, 0)
    w_spec = lambda shape: pl.BlockSpec(shape, const2, pipeline_mode=pl.Buffered(1))
    in_specs = [
        pl.BlockSpec((None, CONV_HALO, D_MODEL), lambda b, i: (b, jnp.maximum(i * hb - 1, 0), 0)),
        pl.BlockSpec((None, tb, D_MODEL), cur),
        pl.BlockSpec((None, CONV_HALO, D_MODEL),
                     lambda b, i: (b, jnp.minimum((i + 1) * hb, s // CONV_HALO - 1), 0)),
        pl.BlockSpec((None, tb, Q_W), cur),
    ]
    for _ in range(2):
        in_specs += [
            pl.BlockSpec((None, 128, KV_W), lambda b, i: (b, jnp.maximum(i * kb - 1, 0), 0)),
            pl.BlockSpec((None, tb, KV_W), cur),
            pl.BlockSpec((None, 128, KV_W), lambda b, i: (b, jnp.minimum((i + 1) * kb, s // 128 - 1), 0)),
        ]
    in_specs += [
        pl.BlockSpec((None, tb, 2 * D_MODEL), cur),
        pl.BlockSpec((None, tb, D_MODEL), cur),
        w_spec((CONV_WIDTH * SUBLANES, D_MODEL)),
        pl.BlockSpec((1, D_MODEL), const2),
        pl.BlockSpec((1, D_MODEL), const2),
        pl.BlockSpec((1, D_MODEL), const2),
        w_spec((D_MODEL, D_MODEL)),
        pl.BlockSpec((N_KV_HEADS, 256, 3 * 128), const3, pipeline_mode=pl.Buffered(1)),
        pl.BlockSpec((N_KV_HEADS, 256, 1), const3),
        w_spec((Q_W, D_MODEL)),
        w_spec((D_MODEL, D_MODEL)),
        pl.BlockSpec((1, D_MODEL), const2),
        pl.BlockSpec((1, D_MODEL), const2),
    ]
    return pl.pallas_call(
        _mix_kernel,
        grid=(nb, nt),
        in_specs=in_specs,
        out_specs=pl.BlockSpec((None, tb, D_MODEL), cur),
        out_shape=jax.ShapeDtypeStruct((nb, s, D_MODEL), jnp.float32),
        scratch_shapes=[
            pltpu.VMEM((tb + 2 * CONV_HALO, D_MODEL), jnp.float32),
            pltpu.VMEM((tb, D_MODEL), jnp.float32),
            pltpu.VMEM((tb, Q_W), jnp.bfloat16),
        ],
        compiler_params=pltpu.CompilerParams(
            dimension_semantics=("arbitrary", "arbitrary"), vmem_limit_bytes=VMEM_LIMIT_BYTES),
        name="mix",
    )(h3, h3, h3, q3, k3, k3, k3, v3, v3, v3, gt3, x3, wdw8, b_dw, cg, cb, wcp, bias_tab,
      sink_tab, wap, wo, g1, b1)


def _route_kernel(x_ref, wr_ref, br_ref, u_ref, aug_ref, bkt_ref, rank_ref, cnt_ref, carry_ref):
    i = pl.program_id(0)
    tr = TILE_R

    @pl.when(i == 0)
    def _():
        carry_ref[...] = jnp.zeros_like(carry_ref)

    x = x_ref[...]
    xh, xl = _split_bf16(x)
    wh, wl = _split_bf16(wr_ref[...])
    lt = _dot_nt(wh, xh) + _dot_nt(wh, xl) + _dot_nt(wl, xh) + br_ref[...]
    le = lt[0:N_EXPERTS]
    lg = lt[N_EXPERTS:N_EXPERTS + SUBLANES]
    iota8 = lax.broadcasted_iota(jnp.int32, (SUBLANES, tr), 0)

    gmax = jnp.max(lg, axis=0, keepdims=True)
    gidx = jnp.min(jnp.where(lg == gmax, iota8, SUBLANES), axis=0, keepdims=True)
    gw = 1.0 / jnp.sum(jnp.exp(lg - gmax), axis=0, keepdims=True)

    ein = le[0:EXPERTS_PER_GROUP]
    for g in range(1, N_GROUPS):
        ein = jnp.where(gidx == g, le[g * EXPERTS_PER_GROUP:(g + 1) * EXPERTS_PER_GROUP], ein)
    ee = jnp.exp(ein - jnp.max(ein, axis=0, keepdims=True))
    prob = ee / jnp.sum(ee, axis=0, keepdims=True)
    p1 = jnp.max(prob, axis=0, keepdims=True)
    i1 = jnp.min(jnp.where(prob == p1, iota8, SUBLANES), axis=0, keepdims=True)
    rest = jnp.where(iota8 == i1, -1.0, prob)
    p2 = jnp.max(rest, axis=0, keepdims=True)
    i2 = jnp.min(jnp.where(rest == p2, iota8, SUBLANES), axis=0, keepdims=True)
    den = p1 + p2
    w1 = gw * (p1 / den)
    w2 = gw * (p2 / den)
    first_lo = i1 < i2
    lo = jnp.minimum(i1, i2)
    hi = jnp.maximum(i1, i2)
    w_lo = jnp.where(first_lo, w1, w2)
    w_hi = jnp.where(first_lo, w2, w1)
    bkt = gidx * (EXPERTS_PER_GROUP * EXPERTS_PER_GROUP) + lo * EXPERTS_PER_GROUP + hi
    bkt_ref[...] = bkt

    onehot = lax.broadcasted_iota(jnp.int32, (N_BUCKETS, tr), 0) == bkt
    ob = jnp.where(onehot, 1.0, 0.0).astype(jnp.bfloat16)
    before = _dot(ob, u_ref[...])
    within = jnp.sum(jnp.where(onehot, before, 0.0), axis=0, keepdims=True)
    carry = carry_ref[...]
    c_hi = jnp.floor(carry * (1.0 / 256.0))
    c_lo = carry - 256.0 * c_hi
    prev = 256.0 * _dot(c_hi.astype(jnp.bfloat16), ob) + _dot(c_lo.astype(jnp.bfloat16), ob)
    rank_ref[...] = (within + prev[0:1]).astype(jnp.int32)
    carry_ref[...] = carry + _dot_nt(jnp.ones((SUBLANES, tr), jnp.bfloat16), ob)

    @pl.when(i == pl.num_programs(0) - 1)
    def _():
        cnt_ref[...] = carry_ref[...]

    for c in range(ROW_CHUNKS):
        aug_ref[pl.ds(c, tr, stride=AUG_ROWS), :] = x[:, c * LANES:(c + 1) * LANES]
    aug_ref[pl.ds(ROW_CHUNKS, tr, stride=AUG_ROWS), :] = jnp.broadcast_to(w_lo, (LANES, tr)).T
    aug_ref[pl.ds(ROW_CHUNKS + 1, tr, stride=AUG_ROWS), :] = jnp.broadcast_to(w_hi, (LANES, tr)).T


def _route(x1, wr, br, upper):
    n = x1.shape[0]
    tr = TILE_R
    const = lambda i: (0, 0)
    return pl.pallas_call(
        _route_kernel,
        grid=(n // tr,),
        in_specs=[
            pl.BlockSpec((tr, D_MODEL), lambda i: (i, 0)),
            pl.BlockSpec(wr.shape, const),
            pl.BlockSpec(br.shape, const),
            pl.BlockSpec((tr, tr), const),
        ],
        out_specs=[
            pl.BlockSpec((tr * AUG_ROWS, LANES), lambda i: (i, 0)),
            pl.BlockSpec((1, tr), lambda i: (0, i)),
            pl.BlockSpec((1, tr), lambda i: (0, i)),
            pl.BlockSpec((SUBLANES, N_BUCKETS), const),
        ],
        out_shape=[
            jax.ShapeDtypeStruct((n * AUG_ROWS, LANES), jnp.float32),
            jax.ShapeDtypeStruct((1, n), jnp.int32),
            jax.ShapeDtypeStruct((1, n), jnp.int32),
            jax.ShapeDtypeStruct((SUBLANES, N_BUCKETS), jnp.float32),
        ],
        scratch_shapes=[pltpu.VMEM((SUBLANES, N_BUCKETS), jnp.float32)],
        compiler_params=pltpu.CompilerParams(
            dimension_semantics=("arbitrary",), vmem_limit_bytes=VMEM_LIMIT_BYTES),
        name="route",
    )(x1, wr, br, upper)


def _plan_kernel(bkt_ref, rank_ref, cnt_ref, u_ref, pos_ref, tb_ref, tv_ref, *, n_tiles_pad):
    tm = TILE_M
    cnt = cnt_ref[...]
    ntl = jnp.floor((cnt + (tm - 1)) * (1.0 / tm))
    tstart = _dot(ntl.astype(jnp.bfloat16), u_ref[...])
    tend = tstart + ntl

    def pick(table, onehot_bf):
        t_hi = jnp.floor(table * (1.0 / 256.0))
        t_lo = table - 256.0 * t_hi
        return (256.0 * _dot(t_hi.astype(jnp.bfloat16), onehot_bf)
                + _dot(t_lo.astype(jnp.bfloat16), onehot_bf))[0:1]

    bkt = bkt_ref[...]
    onehot = lax.broadcasted_iota(jnp.int32, (N_BUCKETS, bkt.shape[1]), 0) == bkt
    ob = jnp.where(onehot, 1.0, 0.0).astype(jnp.bfloat16)
    pos_ref[...] = pick(tstart, ob).astype(jnp.int32) * tm + rank_ref[...]

    eye = (lax.broadcasted_iota(jnp.int32, (N_BUCKETS, N_BUCKETS), 0)
           == lax.broadcasted_iota(jnp.int32, (N_BUCKETS, N_BUCKETS), 1))
    tend_col = jnp.sum(jnp.where(eye, tend[0:1], 0.0), axis=1, keepdims=True)
    tile = lax.broadcasted_iota(jnp.int32, (1, n_tiles_pad), 1).astype(jnp.float32)
    tbk = jnp.sum(jnp.where(tend_col <= tile, 1.0, 0.0), axis=0, keepdims=True)
    tbk_i = tbk.astype(jnp.int32)
    oh_t = lax.broadcasted_iota(jnp.int32, (N_BUCKETS, n_tiles_pad), 0) == tbk_i
    oh_tb = jnp.where(oh_t, 1.0, 0.0).astype(jnp.bfloat16)
    left = pick(cnt, oh_tb) - (tile - pick(tstart, oh_tb)) * tm
    tv_ref[...] = jnp.clip(left, 0.0, float(tm)).astype(jnp.int32)
    tb_ref[...] = jnp.minimum(tbk_i, N_BUCKETS - 1)


def _plan(bkt, rank, cnt, upper, n_tiles_pad):
    n = bkt.shape[1]
    const = lambda i: (0, 0)
    return pl.pallas_call(
        functools.partial(_plan_kernel, n_tiles_pad=n_tiles_pad),
        grid=(n // TILE_P,),
        in_specs=[
            pl.BlockSpec((1, TILE_P), lambda i: (0, i)),
            pl.BlockSpec((1, TILE_P), lambda i: (0, i)),
            pl.BlockSpec((SUBLANES, N_BUCKETS), const),
            pl.BlockSpec((N_BUCKETS, N_BUCKETS), const),
        ],
        out_specs=[
            pl.BlockSpec((1, TILE_P), lambda i: (0, i)),
            pl.BlockSpec((1, n_tiles_pad), const),
            pl.BlockSpec((1, n_tiles_pad), const),
        ],
        out_shape=[
            jax.ShapeDtypeStruct((1, n), jnp.int32),
            jax.ShapeDtypeStruct((1, n_tiles_pad), jnp.int32),
            jax.ShapeDtypeStruct((1, n_tiles_pad), jnp.int32),
        ],
        compiler_params=pltpu.CompilerParams(dimension_semantics=("arbitrary",)),
        name="plan",
    )(bkt, rank, cnt, upper)


def _moe_kernel(pos_ref, tb_ref, tv_ref, x1t_hbm, wgu1_ref, wgu2_ref, wd1_ref, wd2_ref, g2_ref,
                b2_ref, out_hbm, src_ref, xbuf0, xbuf1, obuf0, obuf1, gsem, ssem, *,
                n_tokens, n_tiles):
    i = pl.program_id(0)
    tm = TILE_M
    xbufs = (xbuf0, xbuf1)
    obufs = (obuf0, obuf1)

    def gather_copy(slot, tok, r):
        return pltpu.make_async_copy(x1t_hbm.at[pl.ds(tok * AUG_ROWS, AUG_ROWS), :],
                                     xbufs[slot].at[pl.ds(r * AUG_ROWS, AUG_ROWS), :],
                                     gsem.at[slot])

    def scatter_copy(slot, dst, r):
        return pltpu.make_async_copy(
            obufs[slot].at[pl.ds(r * OUT_PITCH, ROW_CHUNKS), :],
            out_hbm.at[pl.ds(pl.multiple_of(dst * ROW_CHUNKS, ROW_CHUNKS), ROW_CHUNKS), :],
            ssem.at[slot])

    def start_gather(tile, slot):
        base = tile * tm
        last_valid = tv_ref[tile] - 1
        for r in range(tm):
            gather_copy(slot, src_ref[base + jnp.minimum(r, last_valid)], r).start()

    @pl.when(i == 0)
    def _():
        def invert(c, _):
            for u in range(8):
                t = c * 8 + u
                src_ref[pos_ref[t]] = t
            return 0
        lax.fori_loop(0, n_tokens // 8, invert, 0)
        start_gather(0, 0)
        obuf0[...] = jnp.zeros_like(obuf0)
        spare = [pltpu.make_async_copy(
            obuf0.at[pl.ds(0, tm * ROW_CHUNKS), :],
            out_hbm.at[pl.ds((n_tokens + k * tm) * ROW_CHUNKS, tm * ROW_CHUNKS), :],
            ssem.at[k]) for k in range(2)]
        for cp in spare:
            cp.start()
        for cp in spare:
            cp.wait()

    def tile_body(slot):
        nv = tv_ref[i]
        has_next = jnp.logical_and(i + 1 < n_tiles, tv_ref[jnp.minimum(i + 1, n_tiles - 1)] > 0)
        for r in range(tm):
            gather_copy(slot, 0, r).wait()
        start_gather(jnp.where(has_next, i + 1, i), 1 - slot)

        xb_ref = xbufs[slot]
        chunk = lambda c: xb_ref[pl.ds(c, tm, stride=AUG_ROWS), :]
        x = jnp.concatenate([chunk(c) for c in range(ROW_CHUNKS)], axis=1)
        xb = x.astype(jnp.bfloat16)
        acc = jnp.zeros((tm, D_MODEL), jnp.float32)
        for e, (wgu_ref, wd_ref) in enumerate(((wgu1_ref, wd1_ref), (wgu2_ref, wd2_ref))):
            gu = _dot(xb, wgu_ref[...])
            hg = gu[:, 0:EXPERT_HIDDEN]
            hu = gu[:, EXPERT_HIDDEN:]
            hid = (hg * _sigmoid(hg) * hu).astype(jnp.bfloat16)
            y = _dot(hid, wd_ref[...])
            acc = acc + jnp.concatenate([chunk(ROW_CHUNKS + e)] * ROW_CHUNKS, axis=1) * y
        res = _layer_norm(DEEPNORM_ALPHA * x + acc, g2_ref[...], b2_ref[...])
        for c in range(ROW_CHUNKS):
            obufs[slot][pl.ds(c, tm, stride=OUT_PITCH), :] = res[:, c * LANES:(c + 1) * LANES]

        @pl.when(i >= 1)
        def _():
            for r in range(tm):
                scatter_copy(1 - slot, 0, r).wait()

        base = i * tm
        for r in range(tm):
            tok = src_ref[base + jnp.minimum(r, nv - 1)]
            dst = jnp.where(r < nv, tok, n_tokens + slot * tm + r)
            scatter_copy(slot, dst, r).start()

        @pl.when(jnp.logical_not(has_next))
        def _():
            for r in range(tm):
                gather_copy(1 - slot, 0, r).wait()
            for r in range(tm):
                scatter_copy(slot, 0, r).wait()

    used = tv_ref[i] > 0
    for slot in (0, 1):
        @pl.when(jnp.logical_and(used, i % 2 == slot))
        def _(slot=slot):
            tile_body(slot)


def _moe(pos, tbk, tval, x1t, wgu_bf, wd_bf, g2, b2, n_tiles):
    n = pos.shape[0]
    tm = TILE_M
    epg = EXPERTS_PER_GROUP
    e_lo = lambda i, pos, tb, tv: (tb[i] // epg, 0, 0)
    e_hi = lambda i, pos, tb, tv: ((tb[i] // (epg * epg)) * epg + tb[i] % epg, 0, 0)
    const = lambda i, pos, tb, tv: (0, 0)
    grid_spec = pltpu.PrefetchScalarGridSpec(
        num_scalar_prefetch=3,
        grid=(n_tiles,),
        in_specs=[
            pl.BlockSpec(memory_space=pl.ANY),
            pl.BlockSpec((None, D_MODEL, 2 * EXPERT_HIDDEN), e_lo),
            pl.BlockSpec((None, D_MODEL, 2 * EXPERT_HIDDEN), e_hi),
            pl.BlockSpec((None, EXPERT_HIDDEN, D_MODEL), e_lo),
            pl.BlockSpec((None, EXPERT_HIDDEN, D_MODEL), e_hi),
            pl.BlockSpec((1, D_MODEL), const),
            pl.BlockSpec((1, D_MODEL), const),
        ],
        out_specs=pl.BlockSpec(memory_space=pl.ANY),
        scratch_shapes=[
            pltpu.SMEM((n_tiles * tm,), jnp.int32),
            pltpu.VMEM((tm * AUG_ROWS, LANES), jnp.float32),
            pltpu.VMEM((tm * AUG_ROWS, LANES), jnp.float32),
            pltpu.VMEM((tm * OUT_PITCH, LANES), jnp.float32),
            pltpu.VMEM((tm * OUT_PITCH, LANES), jnp.float32),
            pltpu.SemaphoreType.DMA((2,)),
            pltpu.SemaphoreType.DMA((2,)),
        ],
    )
    return pl.pallas_call(
        functools.partial(_moe_kernel, n_tokens=n, n_tiles=n_tiles),
        grid_spec=grid_spec,
        out_shape=jax.ShapeDtypeStruct(((n + 2 * tm) * ROW_CHUNKS, LANES), jnp.float32),
        compiler_params=pltpu.CompilerParams(
            dimension_semantics=("arbitrary",), vmem_limit_bytes=VMEM_LIMIT_BYTES),
        name="moe",
    )(pos, tbk, tval, x1t, wgu_bf, wgu_bf, wd_bf, wd_bf, g2, b2)


def _attention_tables(attn_sink):
    slopes = jnp.exp2(-8.0 * jnp.arange(1, N_Q_HEADS + 1, dtype=jnp.float32) / N_Q_HEADS)
    qi = jnp.arange(128)[:, None]
    kj = jnp.arange(3 * 128)[None, :]
    rel = kj - 128 - qi
    dist = jnp.abs(rel).astype(jnp.float32)
    band = jnp.abs(rel) <= WINDOW
    bias = jnp.where(band[None], -slopes[:, None, None] * dist[None], NEG_INF)
    bias = bias.reshape(N_KV_HEADS, 2 * 128, 3 * 128)
    sink = jnp.broadcast_to(attn_sink.astype(jnp.float32)[:, None], (N_Q_HEADS, 128))
    return bias, sink.reshape(N_KV_HEADS, 2 * 128, 1)


def kernel(x, w_in, b_gate, w_dw, b_dw, conv_ln_g, conv_ln_b, w_conv_proj, attn_sink, w_attn_proj,
           w_out, ln1_g, ln1_b, w_router_group, b_router_group, w_router_expert, b_router_expert,
           w_gate_up, w_down, ln2_g, ln2_b):
    nb, s, d = x.shape
    n = nb * s
    assert d == D_MODEL and s % TILE_B == 0 and n % TILE_A == 0 and n % TILE_R == 0
    assert n % TILE_P == 0 and n % TILE_M == 0
    bf = jnp.bfloat16
    l = 0
    row = lambda a: a[l].reshape(1, -1)

    h, q, k, v, gt = _inproj(x.reshape(n, d), w_in[l].astype(bf), row(b_gate))

    wdw8 = jnp.repeat(w_dw[l], SUBLANES, axis=0)
    bias_tab, sink_tab = _attention_tables(attn_sink[l])
    r3 = lambda a: a.reshape(nb, s, a.shape[-1])
    x1 = _mix(r3(h), r3(q), r3(k), r3(v), r3(gt), x, wdw8, row(b_dw), row(conv_ln_g),
              row(conv_ln_b), w_conv_proj[l].astype(bf), bias_tab, sink_tab,
              w_attn_proj[l].astype(bf), w_out[l].astype(bf), row(ln1_g), row(ln1_b))

    pad = SUBLANES - N_GROUPS
    wr = jnp.concatenate([w_router_expert[l].T, w_router_group[l].T,
                          jnp.zeros((pad, d), jnp.float32)], axis=0)
    br = jnp.concatenate([b_router_expert[l], b_router_group[l],
                          jnp.full((pad,), NEG_INF, jnp.float32)]).reshape(-1, 1)
    upper_r = jnp.triu(jnp.ones((TILE_R, TILE_R), bf), k=1)
    x1t, bkt, rank, cnt = _route(x1.reshape(n, d), wr, br, upper_r)

    n_tiles = n // TILE_M + N_PAIRS
    n_tiles_pad = -(-n_tiles // LANES) * LANES
    upper_b = jnp.triu(jnp.ones((N_BUCKETS, N_BUCKETS), bf), k=1)
    pos, tbk, tval = _plan(bkt, rank, cnt, upper_b, n_tiles_pad)

    out_t = _moe(pos.reshape(n), tbk.reshape(-1), tval.reshape(-1), x1t, w_gate_up[l].astype(bf),
                 w_down[l].astype(bf), row(ln2_g), row(ln2_b), n_tiles)
    return out_t[:n * ROW_CHUNKS].reshape(nb, s, d)
```

```python
import functools

import jax
import jax.numpy as jnp
from jax import lax
from jax.experimental import pallas as pl
from jax.experimental.pallas import tpu as pltpu

D_MODEL = 1024
N_Q_HEADS = 8
N_KV_HEADS = 4
HEAD_DIM = 128
WINDOW = 128
NEG_INF = -1e30
CONV_WIDTH = 31
CONV_PAD = (CONV_WIDTH - 1) // 2
Q_W = N_Q_HEADS * HEAD_DIM
KV_W = N_KV_HEADS * HEAD_DIM
Q_OFF = 2 * D_MODEL
K_OFF = Q_OFF + Q_W
V_OFF = K_OFF + KV_W
G_OFF = V_OFF + KV_W
IN_WIDTH = G_OFF + 2 * D_MODEL
N_GROUPS = 4
EXPERTS_PER_GROUP = 8
N_EXPERTS = N_GROUPS * EXPERTS_PER_GROUP
EXPERT_HIDDEN = D_MODEL // 4
LN_EPS = 1e-5
DEPTH = 1
DEEPNORM_ALPHA = (2.0 * DEPTH) ** 0.25

LANES = 128
SUBLANES = 8
VMEM_LIMIT_BYTES = 56 * 1024 * 1024

TILE_A = 512
TILE_B = 256
CONV_HALO = 16
CONV_ROWS = 128
CONV_STEPS_PER_TURN = 4
TILE_R = 512
TILE_P = 2048
TILE_M = 256
ISSUE_GROUPS = 4
N_BUCKETS = N_GROUPS * EXPERTS_PER_GROUP * EXPERTS_PER_GROUP
N_PAIRS = N_GROUPS * (EXPERTS_PER_GROUP * (EXPERTS_PER_GROUP - 1) // 2)
ROW_CHUNKS = D_MODEL // LANES
AUG_ROWS = ROW_CHUNKS + 2
OUT_PITCH = ROW_CHUNKS + 1

_NT = (((1,), (1,)), ((), ()))


def _dot(a, b):
    return jnp.dot(a, b, preferred_element_type=jnp.float32)


def _dot_nt(a, b):
    return lax.dot_general(a, b, _NT, preferred_element_type=jnp.float32)


def _sigmoid(x):
    return 1.0 / (1.0 + jnp.exp(-x))


def _layer_norm(y, g, b):
    mu = jnp.mean(y, axis=-1, keepdims=True)
    d = y - mu
    var = jnp.mean(d * d, axis=-1, keepdims=True)
    return d * lax.rsqrt(var + LN_EPS) * g + b


def _split_bf16(x):
    hi = x.astype(jnp.bfloat16)
    lo = (x - hi.astype(jnp.float32)).astype(jnp.bfloat16)
    return hi, lo


def _inproj_kernel(x_ref, w_ref, bg_ref, h_ref, q_ref, k_ref, v_ref, gt_ref):
    xb = x_ref[...].astype(jnp.bfloat16)
    a = _dot(xb, w_ref[:, 0:D_MODEL])
    g = _dot(xb, w_ref[:, D_MODEL:Q_OFF])
    h_ref[...] = a * _sigmoid(g)
    q = _dot(xb, w_ref[:, Q_OFF:K_OFF]) * (HEAD_DIM ** -0.5)
    q_ref[...] = q.astype(jnp.bfloat16)
    k_ref[...] = _dot(xb, w_ref[:, K_OFF:V_OFF]).astype(jnp.bfloat16)
    v_ref[...] = _dot(xb, w_ref[:, V_OFF:G_OFF]).astype(jnp.bfloat16)
    gt = _dot(xb, w_ref[:, G_OFF:IN_WIDTH]) + bg_ref[...]
    gt_ref[...] = _sigmoid(gt)


def _inproj(x2, w_in_bf, b_gate):
    n = x2.shape[0]
    const = lambda i: (0, 0)
    row = lambda i: (i, 0)
    return pl.pallas_call(
        _inproj_kernel,
        grid=(n // TILE_A,),
        in_specs=[
            pl.BlockSpec((TILE_A, D_MODEL), row),
            pl.BlockSpec((D_MODEL, IN_WIDTH), const, pipeline_mode=pl.Buffered(1)),
            pl.BlockSpec((1, 2 * D_MODEL), const),
        ],
        out_specs=[
            pl.BlockSpec((TILE_A, D_MODEL), row),
            pl.BlockSpec((TILE_A, Q_W), row),
            pl.BlockSpec((TILE_A, KV_W), row),
            pl.BlockSpec((TILE_A, KV_W), row),
            pl.BlockSpec((TILE_A, 2 * D_MODEL), row),
        ],
        out_shape=[
            jax.ShapeDtypeStruct((n, D_MODEL), jnp.float32),
            jax.ShapeDtypeStruct((n, Q_W), jnp.bfloat16),
            jax.ShapeDtypeStruct((n, KV_W), jnp.bfloat16),
            jax.ShapeDtypeStruct((n, KV_W), jnp.bfloat16),
            jax.ShapeDtypeStruct((n, 2 * D_MODEL), jnp.float32),
        ],
        compiler_params=pltpu.CompilerParams(
            dimension_semantics=("arbitrary",), vmem_limit_bytes=VMEM_LIMIT_BYTES),
        name="inproj",
    )(x2, w_in_bf, b_gate)


def _conv_block(hs_ref, wdw_ref, cv_ref, row0, lane0, rows):
    ng = rows // SUBLANES
    nw = ng + 2 * CONV_HALO // SUBLANES
    lanes = slice(lane0, lane0 + LANES)
    win = [hs_ref[row0 + SUBLANES * g: row0 + SUBLANES * (g + 1), lanes] for g in range(nw)]
    sub = lax.broadcasted_iota(jnp.int32, (SUBLANES, LANES), 0)
    acc = [jnp.zeros((SUBLANES, LANES), jnp.float32) for _ in range(ng)]
    for s in range(1, SUBLANES + 1):
        if s == SUBLANES:
            shifted = win[1:]
        else:
            rolled = [pltpu.roll(w, SUBLANES - s, axis=0) for w in win]
            keep = sub < (SUBLANES - s)
            shifted = [jnp.where(keep, rolled[g], rolled[g + 1]) for g in range(nw - 1)]
        for a in range((CONV_WIDTH + SUBLANES - 1) // SUBLANES):
            j = s - 1 + SUBLANES * a
            if j >= CONV_WIDTH:
                continue
            wv = wdw_ref[SUBLANES * j: SUBLANES * (j + 1), lanes]
            for g in range(ng):
                acc[g] = acc[g] + wv * shifted[g + a]
        if s == SUBLANES:
            for g, v in enumerate(acc):
                cv_ref[row0 + SUBLANES * g: row0 + SUBLANES * (g + 1), lanes] = v
        yield


def _chain(gens):
    for g in gens:
        yield from g


def _round_robin(streams):
    live = [[g, n] for g, n in streams]
    while live:
        for item in list(live):
            for _ in range(item[1]):
                try:
                    next(item[0])
                except StopIteration:
                    live.remove(item)
                    break


def _mix_kernel(hl_ref, hc_ref, hr_ref, q_ref, kl_ref, kc_ref, kr_ref, vl_ref, vc_ref, vr_ref,
                gt_ref, x_ref, wdw_ref, bdw_ref, cg_ref, cb_ref, wcp_ref, bias_ref, sink_ref,
                wap_ref, wo_ref, g1_ref, b1_ref, x1_ref, hs_ref, cv_ref, at_ref):
    i = pl.program_id(1)
    first = i == 0
    last = i == pl.num_programs(1) - 1
    tb = TILE_B

    hs_ref[0:CONV_HALO, :] = jnp.where(first, 0.0, hl_ref[...])
    hs_ref[CONV_HALO:CONV_HALO + tb, :] = hc_ref[...]
    hs_ref[CONV_HALO + tb:, :] = jnp.where(last, 0.0, hr_ref[...])
    conv_stream = _chain(_conv_block(hs_ref, wdw_ref, cv_ref, r0, lc * LANES, CONV_ROWS)
                         for r0 in range(0, tb, CONV_ROWS) for lc in range(D_MODEL // LANES))

    nq = tb // 128
    neg_l = jnp.where(first, NEG_INF, 0.0)
    neg_r = jnp.where(last, NEG_INF, 0.0)
    lane = lax.broadcasted_iota(jnp.int32, (1, 3 * 128), 1)
    edge_l = jnp.where(lane < 128, neg_l, 0.0)
    edge_r = jnp.where(lane >= 256, neg_r, 0.0)

    def attn_task(j, g):
        rs = slice(j * 128, (j + 1) * 128)
        cs = slice(g * HEAD_DIM, (g + 1) * HEAD_DIM)
        if j == 0:
            k0, v0 = kl_ref[:, cs], vl_ref[:, cs]
        else:
            k0, v0 = kc_ref[(j - 1) * 128:j * 128, cs], vc_ref[(j - 1) * 128:j * 128, cs]
        if j == nq - 1:
            k2, v2 = kr_ref[:, cs], vr_ref[:, cs]
        else:
            k2, v2 = kc_ref[(j + 1) * 128:(j + 2) * 128, cs], vc_ref[(j + 1) * 128:(j + 2) * 128, cs]
        k3 = jnp.concatenate([k0, kc_ref[rs, cs], k2], axis=0)
        v3 = jnp.concatenate([v0, vc_ref[rs, cs], v2], axis=0)
        h0 = 2 * g
        q2 = jnp.concatenate([q_ref[rs, h0 * HEAD_DIM:(h0 + 1) * HEAD_DIM],
                              q_ref[rs, (h0 + 1) * HEAD_DIM:(h0 + 2) * HEAD_DIM]], axis=0)
        s = _dot_nt(q2, k3) + bias_ref[g]
        if j == 0:
            s = s + edge_l
        if j == nq - 1:
            s = s + edge_r
        yield
        sk = sink_ref[g]
        m = jnp.maximum(jnp.max(s, axis=1, keepdims=True), sk)
        p = jnp.exp(s - m)
        yield
        den = jnp.sum(p, axis=1, keepdims=True) + jnp.exp(sk - m)
        o = _dot(p.astype(jnp.bfloat16), v3)
        yield
        o = o / den
        at_ref[rs, h0 * HEAD_DIM:(h0 + 1) * HEAD_DIM] = o[0:128].astype(jnp.bfloat16)
        at_ref[rs, (h0 + 1) * HEAD_DIM:(h0 + 2) * HEAD_DIM] = o[128:256].astype(jnp.bfloat16)
        yield

    attn_tasks = [(j, g) for j in range(nq) for g in range(N_KV_HEADS)]
    _round_robin([(_chain(attn_task(j, g) for j, g in attn_tasks[0::2]), 1),
                  (conv_stream, CONV_STEPS_PER_TURN),
                  (_chain(attn_task(j, g) for j, g in attn_tasks[1::2]), 1),
                  (conv_stream, CONV_STEPS_PER_TURN)])

    c = _layer_norm(cv_ref[...] + bdw_ref[...], cg_ref[...], cb_ref[...])
    c = (c * _sigmoid(c)).astype(jnp.bfloat16)
    conv_out = _dot(c, wcp_ref[...])
    attn_out = _dot(at_ref[...], wap_ref[...])

    merged = gt_ref[:, 0:D_MODEL] * conv_out + gt_ref[:, D_MODEL:] * attn_out
    mixed = _dot(merged.astype(jnp.bfloat16), wo_ref[...])
    x1_ref[...] = _layer_norm(DEEPNORM_ALPHA * x_ref[...] + mixed, g1_ref[...], b1_ref[...])


def _mix(h3, q3, k3, v3, gt3, x3, wdw8, b_dw, cg, cb, wcp, bias_tab, sink_tab, wap, wo, g1, b1):
    nb, s, _ = x3.shape
    tb = TILE_B
    nt = s // tb
    hb = tb // CONV_HALO
    kb = tb // 128
    const2 = lambda b, i: (0, 0)
    const3 = lambda b, i: (0, 0, 0)
    cur = lambda b, i: (b, i, 0)
    w_spec = lambda shape: pl.BlockSpec(shape, const2, pipeline_mode=pl.Buffered(1))
    in_specs = [
        pl.BlockSpec((None, CONV_HALO, D_MODEL), lambda b, i: (b, jnp.maximum(i * hb - 1, 0), 0)),
        pl.BlockSpec((None, tb, D_MODEL), cur),
        pl.BlockSpec((None, CONV_HALO, D_MODEL),
                     lambda b, i: (b, jnp.minimum((i + 1) * hb, s // CONV_HALO - 1), 0)),
        pl.BlockSpec((None, tb, Q_W), cur),
    ]
    for _ in range(2):
        in_specs += [
            pl.BlockSpec((None, 128, KV_W), lambda b, i: (b, jnp.maximum(i * kb - 1, 0), 0)),
            pl.BlockSpec((None, tb, KV_W), cur),
            pl.BlockSpec((None, 128, KV_W), lambda b, i: (b, jnp.minimum((i + 1) * kb, s // 128 - 1), 0)),
        ]
    in_specs += [
        pl.BlockSpec((None, tb, 2 * D_MODEL), cur),
        pl.BlockSpec((None, tb, D_MODEL), cur),
        w_spec((CONV_WIDTH * SUBLANES, D_MODEL)),
        pl.BlockSpec((1, D_MODEL), const2),
        pl.BlockSpec((1, D_MODEL), const2),
        pl.BlockSpec((1, D_MODEL), const2),
        w_spec((D_MODEL, D_MODEL)),
        pl.BlockSpec((N_KV_HEADS, 256, 3 * 128), const3, pipeline_mode=pl.Buffered(1)),
        pl.BlockSpec((N_KV_HEADS, 256, 1), const3),
        w_spec((Q_W, D_MODEL)),
        w_spec((D_MODEL, D_MODEL)),
        pl.BlockSpec((1, D_MODEL), const2),
        pl.BlockSpec((1, D_MODEL), const2),
    ]
    return pl.pallas_call(
        _mix_kernel,
        grid=(nb, nt),
        in_specs=in_specs,
        out_specs=pl.BlockSpec((None, tb, D_MODEL), cur),
        out_shape=jax.ShapeDtypeStruct((nb, s, D_MODEL), jnp.float32),
        scratch_shapes=[
            pltpu.VMEM((tb + 2 * CONV_HALO, D_MODEL), jnp.float32),
            pltpu.VMEM((tb, D_MODEL), jnp.float32),
            pltpu.VMEM((tb, Q_W), jnp.bfloat16),
        ],
        compiler_params=pltpu.CompilerParams(
            dimension_semantics=("arbitrary", "arbitrary"), vmem_limit_bytes=VMEM_LIMIT_BYTES),
        name="mix",
    )(h3, h3, h3, q3, k3, k3, k3, v3, v3, v3, gt3, x3, wdw8, b_dw, cg, cb, wcp, bias_tab,
      sink_tab, wap, wo, g1, b1)


def _route_kernel(x_ref, wr_ref, br_ref, u_ref, aug_ref, bkt_ref, rank_ref, cnt_ref, carry_ref):
    i = pl.program_id(0)
    tr = TILE_R

    @pl.when(i == 0)
    def _():
        carry_ref[...] = jnp.zeros_like(carry_ref)

    x = x_ref[...]
    xh, xl = _split_bf16(x)
    wh, wl = _split_bf16(wr_ref[...])
    lt = _dot_nt(wh, xh) + _dot_nt(wh, xl) + _dot_nt(wl, xh) + br_ref[...]
    le = lt[0:N_EXPERTS]
    lg = lt[N_EXPERTS:N_EXPERTS + SUBLANES]
    iota8 = lax.broadcasted_iota(jnp.int32, (SUBLANES, tr), 0)

    gmax = jnp.max(lg, axis=0, keepdims=True)
    gidx = jnp.min(jnp.where(lg == gmax, iota8, SUBLANES), axis=0, keepdims=True)
    gw = 1.0 / jnp.sum(jnp.exp(lg - gmax), axis=0, keepdims=True)

    ein = le[0:EXPERTS_PER_GROUP]
    for g in range(1, N_GROUPS):
        ein = jnp.where(gidx == g, le[g * EXPERTS_PER_GROUP:(g + 1) * EXPERTS_PER_GROUP], ein)
    ee = jnp.exp(ein - jnp.max(ein, axis=0, keepdims=True))
    prob = ee / jnp.sum(ee, axis=0, keepdims=True)
    p1 = jnp.max(prob, axis=0, keepdims=True)
    i1 = jnp.min(jnp.where(prob == p1, iota8, SUBLANES), axis=0, keepdims=True)
    rest = jnp.where(iota8 == i1, -1.0, prob)
    p2 = jnp.max(rest, axis=0, keepdims=True)
    i2 = jnp.min(jnp.where(rest == p2, iota8, SUBLANES), axis=0, keepdims=True)
    den = p1 + p2
    w1 = gw * (p1 / den)
    w2 = gw * (p2 / den)
    first_lo = i1 < i2
    lo = jnp.minimum(i1, i2)
    hi = jnp.maximum(i1, i2)
    w_lo = jnp.where(first_lo, w1, w2)
    w_hi = jnp.where(first_lo, w2, w1)
    bkt = gidx * (EXPERTS_PER_GROUP * EXPERTS_PER_GROUP) + lo * EXPERTS_PER_GROUP + hi
    bkt_ref[...] = bkt

    onehot = lax.broadcasted_iota(jnp.int32, (N_BUCKETS, tr), 0) == bkt
    ob = jnp.where(onehot, 1.0, 0.0).astype(jnp.bfloat16)
    before = _dot(ob, u_ref[...])
    within = jnp.sum(jnp.where(onehot, before, 0.0), axis=0, keepdims=True)
    carry = carry_ref[...]
    c_hi = jnp.floor(carry * (1.0 / 256.0))
    c_lo = carry - 256.0 * c_hi
    prev = 256.0 * _dot(c_hi.astype(jnp.bfloat16), ob) + _dot(c_lo.astype(jnp.bfloat16), ob)
    rank_ref[...] = (within + prev[0:1]).astype(jnp.int32)
    carry_ref[...] = carry + _dot_nt(jnp.ones((SUBLANES, tr), jnp.bfloat16), ob)

    @pl.when(i == pl.num_programs(0) - 1)
    def _():
        cnt_ref[...] = carry_ref[...]

    for c in range(ROW_CHUNKS):
        aug_ref[pl.ds(c, tr, stride=AUG_ROWS), :] = x[:, c * LANES:(c + 1) * LANES]
    aug_ref[pl.ds(ROW_CHUNKS, tr, stride=AUG_ROWS), :] = jnp.broadcast_to(w_lo, (LANES, tr)).T
    aug_ref[pl.ds(ROW_CHUNKS + 1, tr, stride=AUG_ROWS), :] = jnp.broadcast_to(w_hi, (LANES, tr)).T


def _route(x1, wr, br, upper):
    n = x1.shape[0]
    tr = TILE_R
    const = lambda i: (0, 0)
    return pl.pallas_call(
        _route_kernel,
        grid=(n // tr,),
        in_specs=[
            pl.BlockSpec((tr, D_MODEL), lambda i: (i, 0)),
            pl.BlockSpec(wr.shape, const),
            pl.BlockSpec(br.shape, const),
            pl.BlockSpec((tr, tr), const),
        ],
        out_specs=[
            pl.BlockSpec((tr * AUG_ROWS, LANES), lambda i: (i, 0)),
            pl.BlockSpec((1, tr), lambda i: (0, i)),
            pl.BlockSpec((1, tr), lambda i: (0, i)),
            pl.BlockSpec((SUBLANES, N_BUCKETS), const),
        ],
        out_shape=[
            jax.ShapeDtypeStruct((n * AUG_ROWS, LANES), jnp.float32),
            jax.ShapeDtypeStruct((1, n), jnp.int32),
            jax.ShapeDtypeStruct((1, n), jnp.int32),
            jax.ShapeDtypeStruct((SUBLANES, N_BUCKETS), jnp.float32),
        ],
        scratch_shapes=[pltpu.VMEM((SUBLANES, N_BUCKETS), jnp.float32)],
        compiler_params=pltpu.CompilerParams(
            dimension_semantics=("arbitrary",), vmem_limit_bytes=VMEM_LIMIT_BYTES),
        name="route",
    )(x1, wr, br, upper)


def _plan_kernel(bkt_ref, rank_ref, cnt_ref, u_ref, pos_ref, tb_ref, tv_ref, *, n_tiles_pad):
    tm = TILE_M
    cnt = cnt_ref[...]
    ntl = jnp.floor((cnt + (tm - 1)) * (1.0 / tm))
    tstart = _dot(ntl.astype(jnp.bfloat16), u_ref[...])
    tend = tstart + ntl

    def pick(table, onehot_bf):
        t_hi = jnp.floor(table * (1.0 / 256.0))
        t_lo = table - 256.0 * t_hi
        return (256.0 * _dot(t_hi.astype(jnp.bfloat16), onehot_bf)
                + _dot(t_lo.astype(jnp.bfloat16), onehot_bf))[0:1]

    bkt = bkt_ref[...]
    onehot = lax.broadcasted_iota(jnp.int32, (N_BUCKETS, bkt.shape[1]), 0) == bkt
    ob = jnp.where(onehot, 1.0, 0.0).astype(jnp.bfloat16)
    pos_ref[...] = pick(tstart, ob).astype(jnp.int32) * tm + rank_ref[...]

    eye = (lax.broadcasted_iota(jnp.int32, (N_BUCKETS, N_BUCKETS), 0)
           == lax.broadcasted_iota(jnp.int32, (N_BUCKETS, N_BUCKETS), 1))
    tend_col = jnp.sum(jnp.where(eye, tend[0:1], 0.0), axis=1, keepdims=True)
    tile = lax.broadcasted_iota(jnp.int32, (1, n_tiles_pad), 1).astype(jnp.float32)
    tbk = jnp.sum(jnp.where(tend_col <= tile, 1.0, 0.0), axis=0, keepdims=True)
    tbk_i = tbk.astype(jnp.int32)
    oh_t = lax.broadcasted_iota(jnp.int32, (N_BUCKETS, n_tiles_pad), 0) == tbk_i
    oh_tb = jnp.where(oh_t, 1.0, 0.0).astype(jnp.bfloat16)
    left = pick(cnt, oh_tb) - (tile - pick(tstart, oh_tb)) * tm
    tv_ref[...] = jnp.clip(left, 0.0, float(tm)).astype(jnp.int32)
    tb_ref[...] = jnp.minimum(tbk_i, N_BUCKETS - 1)


def _plan(bkt, rank, cnt, upper, n_tiles_pad):
    n = bkt.shape[1]
    const = lambda i: (0, 0)
    return pl.pallas_call(
        functools.partial(_plan_kernel, n_tiles_pad=n_tiles_pad),
        grid=(n // TILE_P,),
        in_specs=[
            pl.BlockSpec((1, TILE_P), lambda i: (0, i)),
            pl.BlockSpec((1, TILE_P), lambda i: (0, i)),
            pl.BlockSpec((SUBLANES, N_BUCKETS), const),
            pl.BlockSpec((N_BUCKETS, N_BUCKETS), const),
        ],
        out_specs=[
            pl.BlockSpec((1, TILE_P), lambda i: (0, i)),
            pl.BlockSpec((1, n_tiles_pad), const),
            pl.BlockSpec((1, n_tiles_pad), const),
        ],
        out_shape=[
            jax.ShapeDtypeStruct((1, n), jnp.int32),
            jax.ShapeDtypeStruct((1, n_tiles_pad), jnp.int32),
            jax.ShapeDtypeStruct((1, n_tiles_pad), jnp.int32),
        ],
        compiler_params=pltpu.CompilerParams(dimension_semantics=("arbitrary",)),
        name="plan",
    )(bkt, rank, cnt, upper)


def _moe_kernel(pos_ref, tb_ref, tv_ref, x1t_hbm, wgu1_ref, wgu2_ref, wd1_ref, wd2_ref, g2_ref,
                b2_ref, out_hbm, src_ref, xbuf0, xbuf1, obuf0, obuf1, gsem, ssem, *,
                n_tokens, n_tiles):
    i = pl.program_id(0)
    tm = TILE_M
    xbufs = (xbuf0, xbuf1)
    obufs = (obuf0, obuf1)

    def gather_copy(slot, tok, r):
        return pltpu.make_async_copy(x1t_hbm.at[pl.ds(tok * AUG_ROWS, AUG_ROWS), :],
                                     xbufs[slot].at[pl.ds(r * AUG_ROWS, AUG_ROWS), :],
                                     gsem.at[slot])

    def scatter_copy(slot, dst, r):
        return pltpu.make_async_copy(
            obufs[slot].at[pl.ds(r * OUT_PITCH, ROW_CHUNKS), :],
            out_hbm.at[pl.ds(pl.multiple_of(dst * ROW_CHUNKS, ROW_CHUNKS), ROW_CHUNKS), :],
            ssem.at[slot])

    def start_gather(tile, slot):
        base = tile * tm
        last_valid = tv_ref[tile] - 1
        for r in range(tm):
            gather_copy(slot, src_ref[base + jnp.minimum(r, last_valid)], r).start()

    @pl.when(i == 0)
    def _():
        def invert(c, _):
            for u in range(8):
                t = c * 8 + u
                src_ref[pos_ref[t]] = t
            return 0
        lax.fori_loop(0, n_tokens // 8, invert, 0)
        start_gather(0, 0)
        obuf0[...] = jnp.zeros_like(obuf0)
        spare = [pltpu.make_async_copy(
            obuf0.at[pl.ds(0, tm * ROW_CHUNKS), :],
            out_hbm.at[pl.ds((n_tokens + k * tm) * ROW_CHUNKS, tm * ROW_CHUNKS), :],
            ssem.at[k]) for k in range(2)]
        for cp in spare:
            cp.start()
        for cp in spare:
            cp.wait()

    def tile_body(slot, first):
        has_next = jnp.logical_and(i + 1 < n_tiles, tv_ref[jnp.minimum(i + 1, n_tiles - 1)] > 0)
        for r in range(tm):
            gather_copy(slot, 0, r).wait()
        nxt = jnp.where(has_next, i + 1, i)
        nxt_base = nxt * tm
        nxt_last = tv_ref[nxt] - 1
        prev_base = (i - 1) * tm
        prev_nv = tv_ref[jnp.maximum(i - 1, 0)]

        def issue(group):
            rows = range(group * tm // ISSUE_GROUPS, (group + 1) * tm // ISSUE_GROUPS)
            for r in rows:
                tok = src_ref[nxt_base + jnp.minimum(r, nxt_last)]
                gather_copy(1 - slot, tok, r).start(priority=r % 2)
            if not first:
                for r in rows:
                    tok = src_ref[prev_base + jnp.minimum(r, prev_nv - 1)]
                    dst = jnp.where(r < prev_nv, tok, n_tokens + (1 - slot) * tm + r)
                    scatter_copy(1 - slot, dst, r).start(priority=r % 2)

        xb_ref = xbufs[slot]
        chunk = lambda c: xb_ref[pl.ds(c, tm, stride=AUG_ROWS), :]
        x = jnp.concatenate([chunk(c) for c in range(ROW_CHUNKS)], axis=1)
        xb = x.astype(jnp.bfloat16)
        acc = jnp.zeros((tm, D_MODEL), jnp.float32)
        for e, (wgu_ref, wd_ref) in enumerate(((wgu1_ref, wd1_ref), (wgu2_ref, wd2_ref))):
            issue(2 * e)
            gu = _dot(xb, wgu_ref[...])
            hg = gu[:, 0:EXPERT_HIDDEN]
            hu = gu[:, EXPERT_HIDDEN:]
            hid = (hg * _sigmoid(hg) * hu).astype(jnp.bfloat16)
            issue(2 * e + 1)
            y = _dot(hid, wd_ref[...])
            acc = acc + jnp.concatenate([chunk(ROW_CHUNKS + e)] * ROW_CHUNKS, axis=1) * y
        res = _layer_norm(DEEPNORM_ALPHA * x + acc, g2_ref[...], b2_ref[...])

        @pl.when(i >= 2)
        def _():
            for r in range(tm):
                scatter_copy(slot, 0, r).wait()

        for c in range(ROW_CHUNKS):
            obufs[slot][pl.ds(c, tm, stride=OUT_PITCH), :] = res[:, c * LANES:(c + 1) * LANES]

        @pl.when(jnp.logical_not(has_next))
        def _():
            nv = tv_ref[i]
            base = i * tm
            for r in range(tm):
                tok = src_ref[base + jnp.minimum(r, nv - 1)]
                dst = jnp.where(r < nv, tok, n_tokens + slot * tm + r)
                scatter_copy(slot, dst, r).start(priority=r % 2)
            for r in range(tm):
                gather_copy(1 - slot, 0, r).wait()
            if not first:
                for r in range(tm):
                    scatter_copy(1 - slot, 0, r).wait()
            for r in range(tm):
                scatter_copy(slot, 0, r).wait()

    used = tv_ref[i] > 0
    for slot, first, cond in ((0, True, i == 0),
                              (0, False, jnp.logical_and(i > 0, i % 2 == 0)),
                              (1, False, i % 2 == 1)):
        @pl.when(jnp.logical_and(used, cond))
        def _(slot=slot, first=first):
            tile_body(slot, first)


def _moe(pos, tbk, tval, x1t, wgu_bf, wd_bf, g2, b2, n_tiles):
    n = pos.shape[0]
    tm = TILE_M
    epg = EXPERTS_PER_GROUP
    e_lo = lambda i, pos, tb, tv: (tb[i] // epg, 0, 0)
    e_hi = lambda i, pos, tb, tv: ((tb[i] // (epg * epg)) * epg + tb[i] % epg, 0, 0)
    const = lambda i, pos, tb, tv: (0, 0)
    grid_spec = pltpu.PrefetchScalarGridSpec(
        num_scalar_prefetch=3,
        grid=(n_tiles,),
        in_specs=[
            pl.BlockSpec(memory_space=pl.ANY),
            pl.BlockSpec((None, D_MODEL, 2 * EXPERT_HIDDEN), e_lo),
            pl.BlockSpec((None, D_MODEL, 2 * EXPERT_HIDDEN), e_hi),
            pl.BlockSpec((None, EXPERT_HIDDEN, D_MODEL), e_lo),
            pl.BlockSpec((None, EXPERT_HIDDEN, D_MODEL), e_hi),
            pl.BlockSpec((1, D_MODEL), const),
            pl.BlockSpec((1, D_MODEL), const),
        ],
        out_specs=pl.BlockSpec(memory_space=pl.ANY),
        scratch_shapes=[
            pltpu.SMEM((n_tiles * tm,), jnp.int32),
            pltpu.VMEM((tm * AUG_ROWS, LANES), jnp.float32),
            pltpu.VMEM((tm * AUG_ROWS, LANES), jnp.float32),
            pltpu.VMEM((tm * OUT_PITCH, LANES), jnp.float32),
            pltpu.VMEM((tm * OUT_PITCH, LANES), jnp.float32),
            pltpu.SemaphoreType.DMA((2,)),
            pltpu.SemaphoreType.DMA((2,)),
        ],
    )
    return pl.pallas_call(
        functools.partial(_moe_kernel, n_tokens=n, n_tiles=n_tiles),
        grid_spec=grid_spec,
        out_shape=jax.ShapeDtypeStruct(((n + 2 * tm) * ROW_CHUNKS, LANES), jnp.float32),
        compiler_params=pltpu.CompilerParams(
            dimension_semantics=("arbitrary",), vmem_limit_bytes=VMEM_LIMIT_BYTES),
        name="moe",
    )(pos, tbk, tval, x1t, wgu_bf, wgu_bf, wd_bf, wd_bf, g2, b2)


def _attention_tables(attn_sink):
    slopes = jnp.exp2(-8.0 * jnp.arange(1, N_Q_HEADS + 1, dtype=jnp.float32) / N_Q_HEADS)
    qi = jnp.arange(128)[:, None]
    kj = jnp.arange(3 * 128)[None, :]
    rel = kj - 128 - qi
    dist = jnp.abs(rel).astype(jnp.float32)
    band = jnp.abs(rel) <= WINDOW
    bias = jnp.where(band[None], -slopes[:, None, None] * dist[None], NEG_INF)
    bias = bias.reshape(N_KV_HEADS, 2 * 128, 3 * 128)
    sink = jnp.broadcast_to(attn_sink.astype(jnp.float32)[:, None], (N_Q_HEADS, 128))
    return bias, sink.reshape(N_KV_HEADS, 2 * 128, 1)


def kernel(x, w_in, b_gate, w_dw, b_dw, conv_ln_g, conv_ln_b, w_conv_proj, attn_sink, w_attn_proj,
           w_out, ln1_g, ln1_b, w_router_group, b_router_group, w_router_expert, b_router_expert,
           w_gate_up, w_down, ln2_g, ln2_b):
    nb, s, d = x.shape
    n = nb * s
    assert d == D_MODEL and s % TILE_B == 0 and n % TILE_A == 0 and n % TILE_R == 0
    assert n % TILE_P == 0 and n % TILE_M == 0
    bf = jnp.bfloat16
    l = 0
    row = lambda a: a[l].reshape(1, -1)

    h, q, k, v, gt = _inproj(x.reshape(n, d), w_in[l].astype(bf), row(b_gate))

    wdw8 = jnp.repeat(w_dw[l], SUBLANES, axis=0)
    bias_tab, sink_tab = _attention_tables(attn_sink[l])
    r3 = lambda a: a.reshape(nb, s, a.shape[-1])
    x1 = _mix(r3(h), r3(q), r3(k), r3(v), r3(gt), x, wdw8, row(b_dw), row(conv_ln_g),
              row(conv_ln_b), w_conv_proj[l].astype(bf), bias_tab, sink_tab,
              w_attn_proj[l].astype(bf), w_out[l].astype(bf), row(ln1_g), row(ln1_b))

    pad = SUBLANES - N_GROUPS
    wr = jnp.concatenate([w_router_expert[l].T, w_router_group[l].T,
                          jnp.zeros((pad, d), jnp.float32)], axis=0)
    br = jnp.concatenate([b_router_expert[l], b_router_group[l],
                          jnp.full((pad,), NEG_INF, jnp.float32)]).reshape(-1, 1)
    upper_r = jnp.triu(jnp.ones((TILE_R, TILE_R), bf), k=1)
    x1t, bkt, rank, cnt = _route(x1.reshape(n, d), wr, br, upper_r)

    n_tiles = n // TILE_M + N_PAIRS
    n_tiles_pad = -(-n_tiles // LANES) * LANES
    upper_b = jnp.triu(jnp.ones((N_BUCKETS, N_BUCKETS), bf), k=1)
    pos, tbk, tval = _plan(bkt, rank, cnt, upper_b, n_tiles_pad)

    out_t = _moe(pos.reshape(n), tbk.reshape(-1), tval.reshape(-1), x1t, w_gate_up[l].astype(bf),
                 w_down[l].astype(bf), row(ln2_g), row(ln2_b), n_tiles)
    return out_t[:n * ROW_CHUNKS].reshape(nb, s, d)
```

```python
import functools

import jax
import jax.numpy as jnp
from jax import lax
from jax.experimental import pallas as pl
from jax.experimental.pallas import tpu as pltpu

D_MODEL = 1024
N_Q_HEADS = 8
N_KV_HEADS = 4
HEAD_DIM = 128
WINDOW = 128
NEG_INF = -1e30
CONV_WIDTH = 31
CONV_PAD = (CONV_WIDTH - 1) // 2
Q_W = N_Q_HEADS * HEAD_DIM
KV_W = N_KV_HEADS * HEAD_DIM
Q_OFF = 2 * D_MODEL
K_OFF = Q_OFF + Q_W
V_OFF = K_OFF + KV_W
G_OFF = V_OFF + KV_W
IN_WIDTH = G_OFF + 2 * D_MODEL
N_GROUPS = 4
EXPERTS_PER_GROUP = 8
N_EXPERTS = N_GROUPS * EXPERTS_PER_GROUP
EXPERT_HIDDEN = D_MODEL // 4
LN_EPS = 1e-5
DEPTH = 1
DEEPNORM_ALPHA = (2.0 * DEPTH) ** 0.25

LANES = 128
SUBLANES = 8
VMEM_LIMIT_BYTES = 56 * 1024 * 1024

TILE_A = 512
TILE_B = 256
CONV_HALO = 16
CONV_ROWS = 128
CONV_STEPS_PER_TURN = 4
TILE_R = 512
TILE_P = 2048
TILE_D = 512
TILE_M = 320
ISSUE_GROUPS = 4
N_BUCKETS = N_GROUPS * EXPERTS_PER_GROUP * EXPERTS_PER_GROUP
N_PAIRS = N_GROUPS * (EXPERTS_PER_GROUP * (EXPERTS_PER_GROUP - 1) // 2)
ROW_CHUNKS = D_MODEL // LANES
AUG_ROWS = ROW_CHUNKS + 2
OUT_PITCH = ROW_CHUNKS + 1

_NT = (((1,), (1,)), ((), ()))


def _dot(a, b):
    return jnp.dot(a, b, preferred_element_type=jnp.float32)


def _dot_nt(a, b):
    return lax.dot_general(a, b, _NT, preferred_element_type=jnp.float32)


def _sigmoid(x):
    return 1.0 / (1.0 + jnp.exp(-x))


def _layer_norm(y, g, b):
    mu = jnp.mean(y, axis=-1, keepdims=True)
    d = y - mu
    var = jnp.mean(d * d, axis=-1, keepdims=True)
    return d * lax.rsqrt(var + LN_EPS) * g + b


def _split_bf16(x):
    hi = x.astype(jnp.bfloat16)
    lo = (x - hi.astype(jnp.float32)).astype(jnp.bfloat16)
    return hi, lo


def _inproj_kernel(x_ref, w_ref, bg_ref, h_ref, q_ref, k_ref, v_ref, gt_ref):
    xb = x_ref[...].astype(jnp.bfloat16)
    a = _dot(xb, w_ref[:, 0:D_MODEL])
    g = _dot(xb, w_ref[:, D_MODEL:Q_OFF])
    h_ref[...] = a * _sigmoid(g)
    q = _dot(xb, w_ref[:, Q_OFF:K_OFF]) * (HEAD_DIM ** -0.5)
    q_ref[...] = q.astype(jnp.bfloat16)
    k_ref[...] = _dot(xb, w_ref[:, K_OFF:V_OFF]).astype(jnp.bfloat16)
    v_ref[...] = _dot(xb, w_ref[:, V_OFF:G_OFF]).astype(jnp.bfloat16)
    gt = _dot(xb, w_ref[:, G_OFF:IN_WIDTH]) + bg_ref[...]
    gt_ref[...] = _sigmoid(gt)


def _inproj(x2, w_in_bf, b_gate):
    n = x2.shape[0]
    const = lambda i: (0, 0)
    row = lambda i: (i, 0)
    return pl.pallas_call(
        _inproj_kernel,
        grid=(n // TILE_A,),
        in_specs=[
            pl.BlockSpec((TILE_A, D_MODEL), row),
            pl.BlockSpec((D_MODEL, IN_WIDTH), const, pipeline_mode=pl.Buffered(1)),
            pl.BlockSpec((1, 2 * D_MODEL), const),
        ],
        out_specs=[
            pl.BlockSpec((TILE_A, D_MODEL), row),
            pl.BlockSpec((TILE_A, Q_W), row),
            pl.BlockSpec((TILE_A, KV_W), row),
            pl.BlockSpec((TILE_A, KV_W), row),
            pl.BlockSpec((TILE_A, 2 * D_MODEL), row),
        ],
        out_shape=[
            jax.ShapeDtypeStruct((n, D_MODEL), jnp.float32),
            jax.ShapeDtypeStruct((n, Q_W), jnp.bfloat16),
            jax.ShapeDtypeStruct((n, KV_W), jnp.bfloat16),
            jax.ShapeDtypeStruct((n, KV_W), jnp.bfloat16),
            jax.ShapeDtypeStruct((n, 2 * D_MODEL), jnp.float32),
        ],
        compiler_params=pltpu.CompilerParams(
            dimension_semantics=("arbitrary",), vmem_limit_bytes=VMEM_LIMIT_BYTES),
        name="inproj",
    )(x2, w_in_bf, b_gate)


def _conv_block(hs_ref, wdw_ref, cv_ref, row0, lane0, rows):
    ng = rows // SUBLANES
    nw = ng + 2 * CONV_HALO // SUBLANES
    lanes = slice(lane0, lane0 + LANES)
    win = [hs_ref[row0 + SUBLANES * g: row0 + SUBLANES * (g + 1), lanes] for g in range(nw)]
    sub = lax.broadcasted_iota(jnp.int32, (SUBLANES, LANES), 0)
    acc = [jnp.zeros((SUBLANES, LANES), jnp.float32) for _ in range(ng)]
    for s in range(1, SUBLANES + 1):
        if s == SUBLANES:
            shifted = win[1:]
        else:
            rolled = [pltpu.roll(w, SUBLANES - s, axis=0) for w in win]
            keep = sub < (SUBLANES - s)
            shifted = [jnp.where(keep, rolled[g], rolled[g + 1]) for g in range(nw - 1)]
        for a in range((CONV_WIDTH + SUBLANES - 1) // SUBLANES):
            j = s - 1 + SUBLANES * a
            if j >= CONV_WIDTH:
                continue
            wv = wdw_ref[SUBLANES * j: SUBLANES * (j + 1), lanes]
            for g in range(ng):
                acc[g] = acc[g] + wv * shifted[g + a]
        if s == SUBLANES:
            for g, v in enumerate(acc):
                cv_ref[row0 + SUBLANES * g: row0 + SUBLANES * (g + 1), lanes] = v
        yield


def _chain(gens):
    for g in gens:
        yield from g


def _round_robin(streams):
    live = [[g, n] for g, n in streams]
    while live:
        for item in list(live):
            for _ in range(item[1]):
                try:
                    next(item[0])
                except StopIteration:
                    live.remove(item)
                    break


def _mix_kernel(hl_ref, hc_ref, hr_ref, q_ref, kl_ref, kc_ref, kr_ref, vl_ref, vc_ref, vr_ref,
                gt_ref, x_ref, wdw_ref, bdw_ref, cg_ref, cb_ref, wcp_ref, bias_ref, sink_ref,
                wap_ref, wo_ref, g1_ref, b1_ref, x1_ref, hs_ref, cv_ref, at_ref):
    i = pl.program_id(1)
    first = i == 0
    last = i == pl.num_programs(1) - 1
    tb = TILE_B

    hs_ref[0:CONV_HALO, :] = jnp.where(first, 0.0, hl_ref[...])
    hs_ref[CONV_HALO:CONV_HALO + tb, :] = hc_ref[...]
    hs_ref[CONV_HALO + tb:, :] = jnp.where(last, 0.0, hr_ref[...])
    conv_stream = _chain(_conv_block(hs_ref, wdw_ref, cv_ref, r0, lc * LANES, CONV_ROWS)
                         for r0 in range(0, tb, CONV_ROWS) for lc in range(D_MODEL // LANES))

    nq = tb // 128
    neg_l = jnp.where(first, NEG_INF, 0.0)
    neg_r = jnp.where(last, NEG_INF, 0.0)
    lane = lax.broadcasted_iota(jnp.int32, (1, 3 * 128), 1)
    edge_l = jnp.where(lane < 128, neg_l, 0.0)
    edge_r = jnp.where(lane >= 256, neg_r, 0.0)

    def attn_task(j, g):
        rs = slice(j * 128, (j + 1) * 128)
        cs = slice(g * HEAD_DIM, (g + 1) * HEAD_DIM)
        if j == 0:
            k0, v0 = kl_ref[:, cs], vl_ref[:, cs]
        else:
            k0, v0 = kc_ref[(j - 1) * 128:j * 128, cs], vc_ref[(j - 1) * 128:j * 128, cs]
        if j == nq - 1:
            k2, v2 = kr_ref[:, cs], vr_ref[:, cs]
        else:
            k2, v2 = kc_ref[(j + 1) * 128:(j + 2) * 128, cs], vc_ref[(j + 1) * 128:(j + 2) * 128, cs]
        k3 = jnp.concatenate([k0, kc_ref[rs, cs], k2], axis=0)
        v3 = jnp.concatenate([v0, vc_ref[rs, cs], v2], axis=0)
        h0 = 2 * g
        q2 = jnp.concatenate([q_ref[rs, h0 * HEAD_DIM:(h0 + 1) * HEAD_DIM],
                              q_ref[rs, (h0 + 1) * HEAD_DIM:(h0 + 2) * HEAD_DIM]], axis=0)
        s = _dot_nt(q2, k3) + bias_ref[g]
        if j == 0:
            s = s + edge_l
        if j == nq - 1:
            s = s + edge_r
        yield
        sk = sink_ref[g]
        m = jnp.maximum(jnp.max(s, axis=1, keepdims=True), sk)
        p = jnp.exp(s - m)
        yield
        den = jnp.sum(p, axis=1, keepdims=True) + jnp.exp(sk - m)
        o = _dot(p.astype(jnp.bfloat16), v3)
        yield
        o = o / den
        at_ref[rs, h0 * HEAD_DIM:(h0 + 1) * HEAD_DIM] = o[0:128].astype(jnp.bfloat16)
        at_ref[rs, (h0 + 1) * HEAD_DIM:(h0 + 2) * HEAD_DIM] = o[128:256].astype(jnp.bfloat16)
        yield

    attn_tasks = [(j, g) for j in range(nq) for g in range(N_KV_HEADS)]
    _round_robin([(_chain(attn_task(j, g) for j, g in attn_tasks[0::2]), 1),
                  (conv_stream, CONV_STEPS_PER_TURN),
                  (_chain(attn_task(j, g) for j, g in attn_tasks[1::2]), 1),
                  (conv_stream, CONV_STEPS_PER_TURN)])

    c = _layer_norm(cv_ref[...] + bdw_ref[...], cg_ref[...], cb_ref[...])
    c = (c * _sigmoid(c)).astype(jnp.bfloat16)
    conv_out = _dot(c, wcp_ref[...])
    attn_out = _dot(at_ref[...], wap_ref[...])

    merged = gt_ref[:, 0:D_MODEL] * conv_out + gt_ref[:, D_MODEL:] * attn_out
    mixed = _dot(merged.astype(jnp.bfloat16), wo_ref[...])
    x1_ref[...] = _layer_norm(DEEPNORM_ALPHA * x_ref[...] + mixed, g1_ref[...], b1_ref[...])


def _mix(h3, q3, k3, v3, gt3, x3, wdw8, b_dw, cg, cb, wcp, bias_tab, sink_tab, wap, wo, g1, b1):
    nb, s, _ = x3.shape
    tb = TILE_B
    nt = s // tb
    hb = tb // CONV_HALO
    kb = tb // 128
    const2 = lambda b, i: (0, 0)
    const3 = lambda b, i: (0, 0, 0)
    cur = lambda b, i: (b, i, 0)
    w_spec = lambda shape: pl.BlockSpec(shape, const2, pipeline_mode=pl.Buffered(1))
    in_specs = [
        pl.BlockSpec((None, CONV_HALO, D_MODEL), lambda b, i: (b, jnp.maximum(i * hb - 1, 0), 0)),
        pl.BlockSpec((None, tb, D_MODEL), cur),
        pl.BlockSpec((None, CONV_HALO, D_MODEL),
                     lambda b, i: (b, jnp.minimum((i + 1) * hb, s // CONV_HALO - 1), 0)),
        pl.BlockSpec((None, tb, Q_W), cur),
    ]
    for _ in range(2):
        in_specs += [
            pl.BlockSpec((None, 128, KV_W), lambda b, i: (b, jnp.maximum(i * kb - 1, 0), 0)),
            pl.BlockSpec((None, tb, KV_W), cur),
            pl.BlockSpec((None, 128, KV_W), lambda b, i: (b, jnp.minimum((i + 1) * kb, s // 128 - 1), 0)),
        ]
    in_specs += [
        pl.BlockSpec((None, tb, 2 * D_MODEL), cur),
        pl.BlockSpec((None, tb, D_MODEL), cur),
        w_spec((CONV_WIDTH * SUBLANES, D_MODEL)),
        pl.BlockSpec((1, D_MODEL), const2),
        pl.BlockSpec((1, D_MODEL), const2),
        pl.BlockSpec((1, D_MODEL), const2),
        w_spec((D_MODEL, D_MODEL)),
        pl.BlockSpec((N_KV_HEADS, 256, 3 * 128), const3, pipeline_mode=pl.Buffered(1)),
        pl.BlockSpec((N_KV_HEADS, 256, 1), const3),
        w_spec((Q_W, D_MODEL)),
        w_spec((D_MODEL, D_MODEL)),
        pl.BlockSpec((1, D_MODEL), const2),
        pl.BlockSpec((1, D_MODEL), const2),
    ]
    return pl.pallas_call(
        _mix_kernel,
        grid=(nb, nt),
        in_specs=in_specs,
        out_specs=pl.BlockSpec((None, tb, D_MODEL), cur),
        out_shape=jax.ShapeDtypeStruct((nb, s, D_MODEL), jnp.float32),
        scratch_shapes=[
            pltpu.VMEM((tb + 2 * CONV_HALO, D_MODEL), jnp.float32),
            pltpu.VMEM((tb, D_MODEL), jnp.float32),
            pltpu.VMEM((tb, Q_W), jnp.bfloat16),
        ],
        compiler_params=pltpu.CompilerParams(
            dimension_semantics=("arbitrary", "arbitrary"), vmem_limit_bytes=VMEM_LIMIT_BYTES),
        name="mix",
    )(h3, h3, h3, q3, k3, k3, k3, v3, v3, v3, gt3, x3, wdw8, b_dw, cg, cb, wcp, bias_tab,
      sink_tab, wap, wo, g1, b1)


def _route_kernel(x_ref, wr_ref, br_ref, u_ref, aug_ref, bkt_ref, rank_ref, cnt_ref, carry_ref):
    i = pl.program_id(0)
    tr = TILE_R

    @pl.when(i == 0)
    def _():
        carry_ref[...] = jnp.zeros_like(carry_ref)

    x = x_ref[...]
    xh, xl = _split_bf16(x)
    wh, wl = _split_bf16(wr_ref[...])
    lt = _dot_nt(wh, xh) + _dot_nt(wh, xl) + _dot_nt(wl, xh) + br_ref[...]
    le = lt[0:N_EXPERTS]
    lg = lt[N_EXPERTS:N_EXPERTS + SUBLANES]
    iota8 = lax.broadcasted_iota(jnp.int32, (SUBLANES, tr), 0)

    gmax = jnp.max(lg, axis=0, keepdims=True)
    gidx = jnp.min(jnp.where(lg == gmax, iota8, SUBLANES), axis=0, keepdims=True)
    gw = 1.0 / jnp.sum(jnp.exp(lg - gmax), axis=0, keepdims=True)

    ein = le[0:EXPERTS_PER_GROUP]
    for g in range(1, N_GROUPS):
        ein = jnp.where(gidx == g, le[g * EXPERTS_PER_GROUP:(g + 1) * EXPERTS_PER_GROUP], ein)
    ee = jnp.exp(ein - jnp.max(ein, axis=0, keepdims=True))
    prob = ee / jnp.sum(ee, axis=0, keepdims=True)
    p1 = jnp.max(prob, axis=0, keepdims=True)
    i1 = jnp.min(jnp.where(prob == p1, iota8, SUBLANES), axis=0, keepdims=True)
    rest = jnp.where(iota8 == i1, -1.0, prob)
    p2 = jnp.max(rest, axis=0, keepdims=True)
    i2 = jnp.min(jnp.where(rest == p2, iota8, SUBLANES), axis=0, keepdims=True)
    den = p1 + p2
    w1 = gw * (p1 / den)
    w2 = gw * (p2 / den)
    first_lo = i1 < i2
    lo = jnp.minimum(i1, i2)
    hi = jnp.maximum(i1, i2)
    w_lo = jnp.where(first_lo, w1, w2)
    w_hi = jnp.where(first_lo, w2, w1)
    bkt = gidx * (EXPERTS_PER_GROUP * EXPERTS_PER_GROUP) + lo * EXPERTS_PER_GROUP + hi
    bkt_ref[...] = bkt

    onehot = lax.broadcasted_iota(jnp.int32, (N_BUCKETS, tr), 0) == bkt
    ob = jnp.where(onehot, 1.0, 0.0).astype(jnp.bfloat16)
    before = _dot(ob, u_ref[...])
    within = jnp.sum(jnp.where(onehot, before, 0.0), axis=0, keepdims=True)
    carry = carry_ref[...]
    c_hi = jnp.floor(carry * (1.0 / 256.0))
    c_lo = carry - 256.0 * c_hi
    prev = 256.0 * _dot(c_hi.astype(jnp.bfloat16), ob) + _dot(c_lo.astype(jnp.bfloat16), ob)
    rank_ref[...] = (within + prev[0:1]).astype(jnp.int32)
    carry_ref[...] = carry + _dot_nt(jnp.ones((SUBLANES, tr), jnp.bfloat16), ob)

    @pl.when(i == pl.num_programs(0) - 1)
    def _():
        cnt_ref[...] = carry_ref[...]

    for c in range(ROW_CHUNKS):
        aug_ref[pl.ds(c, tr, stride=AUG_ROWS), :] = x[:, c * LANES:(c + 1) * LANES]
    aug_ref[pl.ds(ROW_CHUNKS, tr, stride=AUG_ROWS), :] = jnp.broadcast_to(w_lo, (LANES, tr)).T
    aug_ref[pl.ds(ROW_CHUNKS + 1, tr, stride=AUG_ROWS), :] = jnp.broadcast_to(w_hi, (LANES, tr)).T


def _route(x1, wr, br, upper):
    n = x1.shape[0]
    tr = TILE_R
    const = lambda i: (0, 0)
    return pl.pallas_call(
        _route_kernel,
        grid=(n // tr,),
        in_specs=[
            pl.BlockSpec((tr, D_MODEL), lambda i: (i, 0)),
            pl.BlockSpec(wr.shape, const),
            pl.BlockSpec(br.shape, const),
            pl.BlockSpec((tr, tr), const),
        ],
        out_specs=[
            pl.BlockSpec((tr * AUG_ROWS, LANES), lambda i: (i, 0)),
            pl.BlockSpec((1, tr), lambda i: (0, i)),
            pl.BlockSpec((1, tr), lambda i: (0, i)),
            pl.BlockSpec((SUBLANES, N_BUCKETS), const),
        ],
        out_shape=[
            jax.ShapeDtypeStruct((n * AUG_ROWS, LANES), jnp.float32),
            jax.ShapeDtypeStruct((1, n), jnp.int32),
            jax.ShapeDtypeStruct((1, n), jnp.int32),
            jax.ShapeDtypeStruct((SUBLANES, N_BUCKETS), jnp.float32),
        ],
        scratch_shapes=[pltpu.VMEM((SUBLANES, N_BUCKETS), jnp.float32)],
        compiler_params=pltpu.CompilerParams(
            dimension_semantics=("arbitrary",), vmem_limit_bytes=VMEM_LIMIT_BYTES),
        name="route",
    )(x1, wr, br, upper)


def _plan_kernel(bkt_ref, rank_ref, cnt_ref, u_ref, pos_ref, tb_ref, tv_ref, tr_ref, *,
                 n_tiles_pad):
    tm = TILE_M
    cnt = cnt_ref[...]
    ntl = jnp.floor((cnt + (tm - 0.5)) / tm)
    upper = u_ref[...]
    tstart = _dot(ntl.astype(jnp.bfloat16), upper)
    tend = tstart + ntl
    c_hi = jnp.floor(cnt * (1.0 / 256.0))
    c_lo = cnt - 256.0 * c_hi
    bstart = (256.0 * _dot(c_hi.astype(jnp.bfloat16), upper)
              + _dot(c_lo.astype(jnp.bfloat16), upper))

    def pick(table, onehot_bf):
        t_hi = jnp.floor(table * (1.0 / 256.0))
        t_lo = table - 256.0 * t_hi
        return (256.0 * _dot(t_hi.astype(jnp.bfloat16), onehot_bf)
                + _dot(t_lo.astype(jnp.bfloat16), onehot_bf))[0:1]

    bkt = bkt_ref[...]
    onehot = lax.broadcasted_iota(jnp.int32, (N_BUCKETS, bkt.shape[1]), 0) == bkt
    ob = jnp.where(onehot, 1.0, 0.0).astype(jnp.bfloat16)
    pos_ref[...] = pick(bstart, ob).astype(jnp.int32) + rank_ref[...]

    eye = (lax.broadcasted_iota(jnp.int32, (N_BUCKETS, N_BUCKETS), 0)
           == lax.broadcasted_iota(jnp.int32, (N_BUCKETS, N_BUCKETS), 1))
    tend_col = jnp.sum(jnp.where(eye, tend[0:1], 0.0), axis=1, keepdims=True)
    tile = lax.broadcasted_iota(jnp.int32, (1, n_tiles_pad), 1).astype(jnp.float32)
    tbk = jnp.sum(jnp.where(tend_col <= tile, 1.0, 0.0), axis=0, keepdims=True)
    tbk_i = tbk.astype(jnp.int32)
    oh_t = lax.broadcasted_iota(jnp.int32, (N_BUCKETS, n_tiles_pad), 0) == tbk_i
    oh_tb = jnp.where(oh_t, 1.0, 0.0).astype(jnp.bfloat16)
    done = (tile - pick(tstart, oh_tb)) * tm
    valid = jnp.clip(pick(cnt, oh_tb) - done, 0.0, float(tm))
    tv_ref[...] = valid.astype(jnp.int32)
    tr_ref[...] = jnp.where(valid > 0.0, pick(bstart, oh_tb) + done, 0.0).astype(jnp.int32)
    tb_ref[...] = jnp.minimum(tbk_i, N_BUCKETS - 1)


def _plan(bkt, rank, cnt, upper, n_tiles_pad):
    n = bkt.shape[1]
    const = lambda i: (0, 0)
    tile_spec = pl.BlockSpec((1, n_tiles_pad), const)
    tile_shape = jax.ShapeDtypeStruct((1, n_tiles_pad), jnp.int32)
    return pl.pallas_call(
        functools.partial(_plan_kernel, n_tiles_pad=n_tiles_pad),
        grid=(n // TILE_P,),
        in_specs=[
            pl.BlockSpec((1, TILE_P), lambda i: (0, i)),
            pl.BlockSpec((1, TILE_P), lambda i: (0, i)),
            pl.BlockSpec((SUBLANES, N_BUCKETS), const),
            pl.BlockSpec((N_BUCKETS, N_BUCKETS), const),
        ],
        out_specs=[pl.BlockSpec((1, TILE_P), lambda i: (0, i)), tile_spec, tile_spec, tile_spec],
        out_shape=[jax.ShapeDtypeStruct((1, n), jnp.int32), tile_shape, tile_shape, tile_shape],
        compiler_params=pltpu.CompilerParams(dimension_semantics=("arbitrary",)),
        name="plan",
    )(bkt, rank, cnt, upper)


def _dispatch_kernel(pos_ref, x_ref, xs_hbm, zbuf, sem, zsem, *, n_tokens):
    i = pl.program_id(0)
    td = TILE_D

    @pl.when(i == 0)
    def _():
        zbuf[...] = jnp.zeros_like(zbuf)
        tail = pltpu.make_async_copy(
            zbuf, xs_hbm.at[pl.ds(n_tokens * AUG_ROWS, TILE_M * AUG_ROWS), :], zsem.at[0])
        tail.start()
        tail.wait()

    copies = [pltpu.make_async_copy(
        x_ref.at[pl.ds(r * AUG_ROWS, AUG_ROWS), :],
        xs_hbm.at[pl.ds(pos_ref[i * td + r] * AUG_ROWS, AUG_ROWS), :],
        sem.at[0]) for r in range(td)]
    for r, cp in enumerate(copies):
        cp.start(priority=r % 2)
    for cp in copies:
        cp.wait()


def _dispatch(pos, x1t):
    n = pos.shape[0]
    td = TILE_D
    grid_spec = pltpu.PrefetchScalarGridSpec(
        num_scalar_prefetch=1,
        grid=(n // td,),
        in_specs=[pl.BlockSpec((td * AUG_ROWS, LANES), lambda i, pos: (i, 0))],
        out_specs=pl.BlockSpec(memory_space=pl.ANY),
        scratch_shapes=[
            pltpu.VMEM((TILE_M * AUG_ROWS, LANES), jnp.float32),
            pltpu.SemaphoreType.DMA((1,)),
            pltpu.SemaphoreType.DMA((1,)),
        ],
    )
    return pl.pallas_call(
        functools.partial(_dispatch_kernel, n_tokens=n),
        grid_spec=grid_spec,
        out_shape=jax.ShapeDtypeStruct(((n + TILE_M) * AUG_ROWS, LANES), jnp.float32),
        compiler_params=pltpu.CompilerParams(
            dimension_semantics=("arbitrary",), vmem_limit_bytes=VMEM_LIMIT_BYTES),
        name="dispatch",
    )(pos, x1t)


def _moe_kernel(pos_ref, tb_ref, tv_ref, tr_ref, xs_hbm, wgu1_ref, wgu2_ref, wd1_ref, wd2_ref,
                g2_ref, b2_ref, out_hbm, src_ref, xbuf0, xbuf1, obuf0, obuf1, gsem, ssem, *,
                n_tokens, n_tiles):
    i = pl.program_id(0)
    tm = TILE_M
    xbufs = (xbuf0, xbuf1)
    obufs = (obuf0, obuf1)

    def tile_copy(slot, tile):
        return pltpu.make_async_copy(
            xs_hbm.at[pl.ds(tr_ref[tile] * AUG_ROWS, tm * AUG_ROWS), :], xbufs[slot], gsem.at[slot])

    def scatter_copy(slot, dst, r):
        return pltpu.make_async_copy(
            obufs[slot].at[pl.ds(r * OUT_PITCH, ROW_CHUNKS), :],
            out_hbm.at[pl.ds(pl.multiple_of(dst * ROW_CHUNKS, ROW_CHUNKS), ROW_CHUNKS), :],
            ssem.at[slot])

    def start_scatter(slot, tile, rows):
        row0 = tr_ref[tile]
        nv = tv_ref[tile]
        for r in rows:
            tok = src_ref[row0 + jnp.minimum(r, nv - 1)]
            dst = jnp.where(r < nv, tok, n_tokens + slot * tm + r)
            scatter_copy(slot, dst, r).start(priority=r % 2)

    @pl.when(i == 0)
    def _():
        def invert(c, _):
            for u in range(8):
                t = c * 8 + u
                src_ref[pos_ref[t]] = t
            return 0
        lax.fori_loop(0, n_tokens // 8, invert, 0)
        tile_copy(0, 0).start()
        obuf0[...] = jnp.zeros_like(obuf0)
        spare = [pltpu.make_async_copy(
            obuf0.at[pl.ds(0, tm * ROW_CHUNKS), :],
            out_hbm.at[pl.ds((n_tokens + k * tm) * ROW_CHUNKS, tm * ROW_CHUNKS), :],
            ssem.at[k]) for k in range(2)]
        for cp in spare:
            cp.start()
        for cp in spare:
            cp.wait()

    def tile_body(slot, first):
        has_next = jnp.logical_and(i + 1 < n_tiles, tv_ref[jnp.minimum(i + 1, n_tiles - 1)] > 0)
        tile_copy(slot, i).wait()
        tile_copy(1 - slot, jnp.where(has_next, i + 1, i)).start()

        def issue(group):
            if not first:
                start_scatter(1 - slot, i - 1,
                              range(group * tm // ISSUE_GROUPS, (group + 1) * tm // ISSUE_GROUPS))

        xb_ref = xbufs[slot]
        chunk = lambda c: xb_ref[pl.ds(c, tm, stride=AUG_ROWS), :]
        x = jnp.concatenate([chunk(c) for c in range(ROW_CHUNKS)], axis=1)
        xb = x.astype(jnp.bfloat16)
        acc = jnp.zeros((tm, D_MODEL), jnp.float32)
        for e, (wgu_ref, wd_ref) in enumerate(((wgu1_ref, wd1_ref), (wgu2_ref, wd2_ref))):
            issue(2 * e)
            gu = _dot(xb, wgu_ref[...])
            hg = gu[:, 0:EXPERT_HIDDEN]
            hu = gu[:, EXPERT_HIDDEN:]
            hid = (hg * _sigmoid(hg) * hu).astype(jnp.bfloat16)
            issue(2 * e + 1)
            y = _dot(hid, wd_ref[...])
            acc = acc + jnp.concatenate([chunk(ROW_CHUNKS + e)] * ROW_CHUNKS, axis=1) * y
        res = _layer_norm(DEEPNORM_ALPHA * x + acc, g2_ref[...], b2_ref[...])

        @pl.when(i >= 2)
        def _():
            for r in range(tm):
                scatter_copy(slot, 0, r).wait()

        for c in range(ROW_CHUNKS):
            obufs[slot][pl.ds(c, tm, stride=OUT_PITCH), :] = res[:, c * LANES:(c + 1) * LANES]

        @pl.when(jnp.logical_not(has_next))
        def _():
            start_scatter(slot, i, range(tm))
            tile_copy(1 - slot, i).wait()
            if not first:
                for r in range(tm):
                    scatter_copy(1 - slot, 0, r).wait()
            for r in range(tm):
                scatter_copy(slot, 0, r).wait()

    used = tv_ref[i] > 0
    for slot, first, cond in ((0, True, i == 0),
                              (0, False, jnp.logical_and(i > 0, i % 2 == 0)),
                              (1, False, i % 2 == 1)):
        @pl.when(jnp.logical_and(used, cond))
        def _(slot=slot, first=first):
            tile_body(slot, first)


def _moe(pos, tbk, tval, trow, xs, wgu_bf, wd_bf, g2, b2, n_tiles):
    n = pos.shape[0]
    tm = TILE_M
    epg = EXPERTS_PER_GROUP
    e_lo = lambda i, pos, tb, tv, tr: (tb[i] // epg, 0, 0)
    e_hi = lambda i, pos, tb, tv, tr: ((tb[i] // (epg * epg)) * epg + tb[i] % epg, 0, 0)
    const = lambda i, pos, tb, tv, tr: (0, 0)
    grid_spec = pltpu.PrefetchScalarGridSpec(
        num_scalar_prefetch=4,
        grid=(n_tiles,),
        in_specs=[
            pl.BlockSpec(memory_space=pl.ANY),
            pl.BlockSpec((None, D_MODEL, 2 * EXPERT_HIDDEN), e_lo),
            pl.BlockSpec((None, D_MODEL, 2 * EXPERT_HIDDEN), e_hi),
            pl.BlockSpec((None, EXPERT_HIDDEN, D_MODEL), e_lo),
            pl.BlockSpec((None, EXPERT_HIDDEN, D_MODEL), e_hi),
            pl.BlockSpec((1, D_MODEL), const),
            pl.BlockSpec((1, D_MODEL), const),
        ],
        out_specs=pl.BlockSpec(memory_space=pl.ANY),
        scratch_shapes=[
            pltpu.SMEM((n,), jnp.int32),
            pltpu.VMEM((tm * AUG_ROWS, LANES), jnp.float32),
            pltpu.VMEM((tm * AUG_ROWS, LANES), jnp.float32),
            pltpu.VMEM((tm * OUT_PITCH, LANES), jnp.float32),
            pltpu.VMEM((tm * OUT_PITCH, LANES), jnp.float32),
            pltpu.SemaphoreType.DMA((2,)),
            pltpu.SemaphoreType.DMA((2,)),
        ],
    )
    return pl.pallas_call(
        functools.partial(_moe_kernel, n_tokens=n, n_tiles=n_tiles),
        grid_spec=grid_spec,
        out_shape=jax.ShapeDtypeStruct(((n + 2 * tm) * ROW_CHUNKS, LANES), jnp.float32),
        compiler_params=pltpu.CompilerParams(
            dimension_semantics=("arbitrary",), vmem_limit_bytes=VMEM_LIMIT_BYTES),
        name="moe",
    )(pos, tbk, tval, trow, xs, wgu_bf, wgu_bf, wd_bf, wd_bf, g2, b2)


def _attention_tables(attn_sink):
    slopes = jnp.exp2(-8.0 * jnp.arange(1, N_Q_HEADS + 1, dtype=jnp.float32) / N_Q_HEADS)
    qi = jnp.arange(128)[:, None]
    kj = jnp.arange(3 * 128)[None, :]
    rel = kj - 128 - qi
    dist = jnp.abs(rel).astype(jnp.float32)
    band = jnp.abs(rel) <= WINDOW
    bias = jnp.where(band[None], -slopes[:, None, None] * dist[None], NEG_INF)
    bias = bias.reshape(N_KV_HEADS, 2 * 128, 3 * 128)
    sink = jnp.broadcast_to(attn_sink.astype(jnp.float32)[:, None], (N_Q_HEADS, 128))
    return bias, sink.reshape(N_KV_HEADS, 2 * 128, 1)


def kernel(x, w_in, b_gate, w_dw, b_dw, conv_ln_g, conv_ln_b, w_conv_proj, attn_sink, w_attn_proj,
           w_out, ln1_g, ln1_b, w_router_group, b_router_group, w_router_expert, b_router_expert,
           w_gate_up, w_down, ln2_g, ln2_b):
    nb, s, d = x.shape
    n = nb * s
    assert d == D_MODEL and s % TILE_B == 0 and n % TILE_A == 0 and n % TILE_R == 0
    assert n % TILE_P == 0 and n % TILE_D == 0
    bf = jnp.bfloat16
    l = 0
    row = lambda a: a[l].reshape(1, -1)

    h, q, k, v, gt = _inproj(x.reshape(n, d), w_in[l].astype(bf), row(b_gate))

    wdw8 = jnp.repeat(w_dw[l], SUBLANES, axis=0)
    bias_tab, sink_tab = _attention_tables(attn_sink[l])
    r3 = lambda a: a.reshape(nb, s, a.shape[-1])
    x1 = _mix(r3(h), r3(q), r3(k), r3(v), r3(gt), x, wdw8, row(b_dw), row(conv_ln_g),
              row(conv_ln_b), w_conv_proj[l].astype(bf), bias_tab, sink_tab,
              w_attn_proj[l].astype(bf), w_out[l].astype(bf), row(ln1_g), row(ln1_b))

    pad = SUBLANES - N_GROUPS
    wr = jnp.concatenate([w_router_expert[l].T, w_router_group[l].T,
                          jnp.zeros((pad, d), jnp.float32)], axis=0)
    br = jnp.concatenate([b_router_expert[l], b_router_group[l],
                          jnp.full((pad,), NEG_INF, jnp.float32)]).reshape(-1, 1)
    upper_r = jnp.triu(jnp.ones((TILE_R, TILE_R), bf), k=1)
    x1t, bkt, rank, cnt = _route(x1.reshape(n, d), wr, br, upper_r)

    n_tiles = -(-n // TILE_M) + N_PAIRS
    n_tiles_pad = -(-n_tiles // LANES) * LANES
    upper_b = jnp.triu(jnp.ones((N_BUCKETS, N_BUCKETS), bf), k=1)
    pos, tbk, tval, trow = _plan(bkt, rank, cnt, upper_b, n_tiles_pad)
    pos = pos.reshape(n)

    xs = _dispatch(pos, x1t)
    out_t = _moe(pos, tbk.reshape(-1), tval.reshape(-1), trow.reshape(-1), xs,
                 w_gate_up[l].astype(bf), w_down[l].astype(bf), row(ln2_g), row(ln2_b), n_tiles)
    return out_t[:n * ROW_CHUNKS].reshape(nb, s, d)
```

```python
import functools

import jax
import jax.numpy as jnp
from jax import lax
from jax.experimental import pallas as pl
from jax.experimental.pallas import tpu as pltpu

D_MODEL = 1024
N_Q_HEADS = 8
N_KV_HEADS = 4
HEAD_DIM = 128
WINDOW = 128
NEG_INF = -1e30
CONV_WIDTH = 31
CONV_PAD = (CONV_WIDTH - 1) // 2
Q_W = N_Q_HEADS * HEAD_DIM
KV_W = N_KV_HEADS * HEAD_DIM
Q_OFF = 2 * D_MODEL
K_OFF = Q_OFF + Q_W
V_OFF = K_OFF + KV_W
G_OFF = V_OFF + KV_W
IN_WIDTH = G_OFF + 2 * D_MODEL
N_GROUPS = 4
EXPERTS_PER_GROUP = 8
N_EXPERTS = N_GROUPS * EXPERTS_PER_GROUP
EXPERT_HIDDEN = D_MODEL // 4
LN_EPS = 1e-5
DEPTH = 1
DEEPNORM_ALPHA = (2.0 * DEPTH) ** 0.25

LANES = 128
SUBLANES = 8
VMEM_LIMIT_BYTES = 56 * 1024 * 1024

TILE_A = 512
TILE_B = 256
CONV_HALO = 16
CONV_ROWS = 128
CONV_STEPS_PER_TURN = 4
TILE_R = 512
TILE_P = 2048
TILE_D = 512
TILE_M = 320
SCATTER_GROUP = 32
N_BUCKETS = N_GROUPS * EXPERTS_PER_GROUP * EXPERTS_PER_GROUP
N_PAIRS = N_GROUPS * (EXPERTS_PER_GROUP * (EXPERTS_PER_GROUP - 1) // 2)
ROW_CHUNKS = D_MODEL // LANES
AUG_ROWS = ROW_CHUNKS + 2
_NT = (((1,), (1,)), ((), ()))


def _dot(a, b):
    return jnp.dot(a, b, preferred_element_type=jnp.float32)


def _dot_nt(a, b):
    return lax.dot_general(a, b, _NT, preferred_element_type=jnp.float32)


def _sigmoid(x):
    return 1.0 / (1.0 + jnp.exp(-x))


def _layer_norm(y, g, b):
    mu = jnp.mean(y, axis=-1, keepdims=True)
    d = y - mu
    var = jnp.mean(d * d, axis=-1, keepdims=True)
    return d * lax.rsqrt(var + LN_EPS) * g + b


def _split_bf16(x):
    hi = x.astype(jnp.bfloat16)
    lo = (x - hi.astype(jnp.float32)).astype(jnp.bfloat16)
    return hi, lo


def _inproj_kernel(x_ref, w_ref, bg_ref, h_ref, q_ref, k_ref, v_ref, gt_ref):
    xb = x_ref[...].astype(jnp.bfloat16)
    a = _dot(xb, w_ref[:, 0:D_MODEL])
    g = _dot(xb, w_ref[:, D_MODEL:Q_OFF])
    h_ref[...] = a * _sigmoid(g)
    q = _dot(xb, w_ref[:, Q_OFF:K_OFF]) * (HEAD_DIM ** -0.5)
    q_ref[...] = q.astype(jnp.bfloat16)
    k_ref[...] = _dot(xb, w_ref[:, K_OFF:V_OFF]).astype(jnp.bfloat16)
    v_ref[...] = _dot(xb, w_ref[:, V_OFF:G_OFF]).astype(jnp.bfloat16)
    gt = _dot(xb, w_ref[:, G_OFF:IN_WIDTH]) + bg_ref[...]
    gt_ref[...] = _sigmoid(gt)


def _inproj(x2, w_in_bf, b_gate):
    n = x2.shape[0]
    const = lambda i: (0, 0)
    row = lambda i: (i, 0)
    return pl.pallas_call(
        _inproj_kernel,
        grid=(n // TILE_A,),
        in_specs=[
            pl.BlockSpec((TILE_A, D_MODEL), row),
            pl.BlockSpec((D_MODEL, IN_WIDTH), const, pipeline_mode=pl.Buffered(1)),
            pl.BlockSpec((1, 2 * D_MODEL), const),
        ],
        out_specs=[
            pl.BlockSpec((TILE_A, D_MODEL), row),
            pl.BlockSpec((TILE_A, Q_W), row),
            pl.BlockSpec((TILE_A, KV_W), row),
            pl.BlockSpec((TILE_A, KV_W), row),
            pl.BlockSpec((TILE_A, 2 * D_MODEL), row),
        ],
        out_shape=[
            jax.ShapeDtypeStruct((n, D_MODEL), jnp.float32),
            jax.ShapeDtypeStruct((n, Q_W), jnp.bfloat16),
            jax.ShapeDtypeStruct((n, KV_W), jnp.bfloat16),
            jax.ShapeDtypeStruct((n, KV_W), jnp.bfloat16),
            jax.ShapeDtypeStruct((n, 2 * D_MODEL), jnp.float32),
        ],
        compiler_params=pltpu.CompilerParams(
            dimension_semantics=("arbitrary",), vmem_limit_bytes=VMEM_LIMIT_BYTES),
        name="inproj",
    )(x2, w_in_bf, b_gate)


def _conv_block(hs_ref, wdw_ref, cv_ref, row0, lane0, rows):
    ng = rows // SUBLANES
    nw = ng + 2 * CONV_HALO // SUBLANES
    lanes = slice(lane0, lane0 + LANES)
    win = [hs_ref[row0 + SUBLANES * g: row0 + SUBLANES * (g + 1), lanes] for g in range(nw)]
    sub = lax.broadcasted_iota(jnp.int32, (SUBLANES, LANES), 0)
    acc = [jnp.zeros((SUBLANES, LANES), jnp.float32) for _ in range(ng)]
    for s in range(1, SUBLANES + 1):
        if s == SUBLANES:
            shifted = win[1:]
        else:
            rolled = [pltpu.roll(w, SUBLANES - s, axis=0) for w in win]
            keep = sub < (SUBLANES - s)
            shifted = [jnp.where(keep, rolled[g], rolled[g + 1]) for g in range(nw - 1)]
        for a in range((CONV_WIDTH + SUBLANES - 1) // SUBLANES):
            j = s - 1 + SUBLANES * a
            if j >= CONV_WIDTH:
                continue
            wv = wdw_ref[SUBLANES * j: SUBLANES * (j + 1), lanes]
            for g in range(ng):
                acc[g] = acc[g] + wv * shifted[g + a]
        if s == SUBLANES:
            for g, v in enumerate(acc):
                cv_ref[row0 + SUBLANES * g: row0 + SUBLANES * (g + 1), lanes] = v
        yield


def _chain(gens):
    for g in gens:
        yield from g


def _round_robin(streams):
    live = [[g, n] for g, n in streams]
    while live:
        for item in list(live):
            for _ in range(item[1]):
                try:
                    next(item[0])
                except StopIteration:
                    live.remove(item)
                    break


def _mix_kernel(hl_ref, hc_ref, hr_ref, q_ref, kl_ref, kc_ref, kr_ref, vl_ref, vc_ref, vr_ref,
                gt_ref, x_ref, wdw_ref, bdw_ref, cg_ref, cb_ref, wcp_ref, bias_ref, sink_ref,
                wap_ref, wo_ref, g1_ref, b1_ref, x1_ref, hs_ref, cv_ref, at_ref):
    i = pl.program_id(1)
    first = i == 0
    last = i == pl.num_programs(1) - 1
    tb = TILE_B

    hs_ref[0:CONV_HALO, :] = jnp.where(first, 0.0, hl_ref[...])
    hs_ref[CONV_HALO:CONV_HALO + tb, :] = hc_ref[...]
    hs_ref[CONV_HALO + tb:, :] = jnp.where(last, 0.0, hr_ref[...])
    conv_stream = _chain(_conv_block(hs_ref, wdw_ref, cv_ref, r0, lc * LANES, CONV_ROWS)
                         for r0 in range(0, tb, CONV_ROWS) for lc in range(D_MODEL // LANES))

    nq = tb // 128
    neg_l = jnp.where(first, NEG_INF, 0.0)
    neg_r = jnp.where(last, NEG_INF, 0.0)
    lane = lax.broadcasted_iota(jnp.int32, (1, 3 * 128), 1)
    edge_l = jnp.where(lane < 128, neg_l, 0.0)
    edge_r = jnp.where(lane >= 256, neg_r, 0.0)

    def attn_task(j, g):
        rs = slice(j * 128, (j + 1) * 128)
        cs = slice(g * HEAD_DIM, (g + 1) * HEAD_DIM)
        if j == 0:
            k0, v0 = kl_ref[:, cs], vl_ref[:, cs]
        else:
            k0, v0 = kc_ref[(j - 1) * 128:j * 128, cs], vc_ref[(j - 1) * 128:j * 128, cs]
        if j == nq - 1:
            k2, v2 = kr_ref[:, cs], vr_ref[:, cs]
        else:
            k2, v2 = kc_ref[(j + 1) * 128:(j + 2) * 128, cs], vc_ref[(j + 1) * 128:(j + 2) * 128, cs]
        k3 = jnp.concatenate([k0, kc_ref[rs, cs], k2], axis=0)
        v3 = jnp.concatenate([v0, vc_ref[rs, cs], v2], axis=0)
        h0 = 2 * g
        q2 = jnp.concatenate([q_ref[rs, h0 * HEAD_DIM:(h0 + 1) * HEAD_DIM],
                              q_ref[rs, (h0 + 1) * HEAD_DIM:(h0 + 2) * HEAD_DIM]], axis=0)
        s = _dot_nt(q2, k3) + bias_ref[g]
        if j == 0:
            s = s + edge_l
        if j == nq - 1:
            s = s + edge_r
        yield
        sk = sink_ref[g]
        m = jnp.maximum(jnp.max(s, axis=1, keepdims=True), sk)
        p = jnp.exp(s - m)
        yield
        den = jnp.sum(p, axis=1, keepdims=True) + jnp.exp(sk - m)
        o = _dot(p.astype(jnp.bfloat16), v3)
        yield
        o = o / den
        at_ref[rs, h0 * HEAD_DIM:(h0 + 1) * HEAD_DIM] = o[0:128].astype(jnp.bfloat16)
        at_ref[rs, (h0 + 1) * HEAD_DIM:(h0 + 2) * HEAD_DIM] = o[128:256].astype(jnp.bfloat16)
        yield

    attn_tasks = [(j, g) for j in range(nq) for g in range(N_KV_HEADS)]
    _round_robin([(_chain(attn_task(j, g) for j, g in attn_tasks[0::2]), 1),
                  (conv_stream, CONV_STEPS_PER_TURN),
                  (_chain(attn_task(j, g) for j, g in attn_tasks[1::2]), 1),
                  (conv_stream, CONV_STEPS_PER_TURN)])

    c = _layer_norm(cv_ref[...] + bdw_ref[...], cg_ref[...], cb_ref[...])
    c = (c * _sigmoid(c)).astype(jnp.bfloat16)
    conv_out = _dot(c, wcp_ref[...])
    attn_out = _dot(at_ref[...], wap_ref[...])

    merged = gt_ref[:, 0:D_MODEL] * conv_out + gt_ref[:, D_MODEL:] * attn_out
    mixed = _dot(merged.astype(jnp.bfloat16), wo_ref[...])
    x1_ref[...] = _layer_norm(DEEPNORM_ALPHA * x_ref[...] + mixed, g1_ref[...], b1_ref[...])


def _mix(h3, q3, k3, v3, gt3, x3, wdw8, b_dw, cg, cb, wcp, bias_tab, sink_tab, wap, wo, g1, b1):
    nb, s, _ = x3.shape
    tb = TILE_B
    nt = s // tb
    hb = tb // CONV_HALO
    kb = tb // 128
    const2 = lambda b, i: (0, 0)
    const3 = lambda b, i: (0, 0, 0)
    cur = lambda b, i: (b, i, 0)
    w_spec = lambda shape: pl.BlockSpec(shape, const2, pipeline_mode=pl.Buffered(1))
    in_specs = [
        pl.BlockSpec((None, CONV_HALO, D_MODEL), lambda b, i: (b, jnp.maximum(i * hb - 1, 0), 0)),
        pl.BlockSpec((None, tb, D_MODEL), cur),
        pl.BlockSpec((None, CONV_HALO, D_MODEL),
                     lambda b, i: (b, jnp.minimum((i + 1) * hb, s // CONV_HALO - 1), 0)),
        pl.BlockSpec((None, tb, Q_W), cur),
    ]
    for _ in range(2):
        in_specs += [
            pl.BlockSpec((None, 128, KV_W), lambda b, i: (b, jnp.maximum(i * kb - 1, 0), 0)),
            pl.BlockSpec((None, tb, KV_W), cur),
            pl.BlockSpec((None, 128, KV_W), lambda b, i: (b, jnp.minimum((i + 1) * kb, s // 128 - 1), 0)),
        ]
    in_specs += [
        pl.BlockSpec((None, tb, 2 * D_MODEL), cur),
        pl.BlockSpec((None, tb, D_MODEL), cur),
        w_spec((CONV_WIDTH * SUBLANES, D_MODEL)),
        pl.BlockSpec((1, D_MODEL), const2),
        pl.BlockSpec((1, D_MODEL), const2),
        pl.BlockSpec((1, D_MODEL), const2),
        w_spec((D_MODEL, D_MODEL)),
        pl.BlockSpec((N_KV_HEADS, 256, 3 * 128), const3, pipeline_mode=pl.Buffered(1)),
        pl.BlockSpec((N_KV_HEADS, 256, 1), const3),
        w_spec((Q_W, D_MODEL)),
        w_spec((D_MODEL, D_MODEL)),
        pl.BlockSpec((1, D_MODEL), const2),
        pl.BlockSpec((1, D_MODEL), const2),
    ]
    return pl.pallas_call(
        _mix_kernel,
        grid=(nb, nt),
        in_specs=in_specs,
        out_specs=pl.BlockSpec((None, tb, D_MODEL), cur),
        out_shape=jax.ShapeDtypeStruct((nb, s, D_MODEL), jnp.float32),
        scratch_shapes=[
            pltpu.VMEM((tb + 2 * CONV_HALO, D_MODEL), jnp.float32),
            pltpu.VMEM((tb, D_MODEL), jnp.float32),
            pltpu.VMEM((tb, Q_W), jnp.bfloat16),
        ],
        compiler_params=pltpu.CompilerParams(
            dimension_semantics=("arbitrary", "arbitrary"), vmem_limit_bytes=VMEM_LIMIT_BYTES),
        name="mix",
    )(h3, h3, h3, q3, k3, k3, k3, v3, v3, v3, gt3, x3, wdw8, b_dw, cg, cb, wcp, bias_tab,
      sink_tab, wap, wo, g1, b1)


def _route_kernel(x_ref, wr_ref, br_ref, u_ref, aug_ref, bkt_ref, rank_ref, cnt_ref, carry_ref):
    i = pl.program_id(0)
    tr = TILE_R

    @pl.when(i == 0)
    def _():
        carry_ref[...] = jnp.zeros_like(carry_ref)

    x = x_ref[...]
    xh, xl = _split_bf16(x)
    wh, wl = _split_bf16(wr_ref[...])
    lt = _dot_nt(wh, xh) + _dot_nt(wh, xl) + _dot_nt(wl, xh) + br_ref[...]
    le = lt[0:N_EXPERTS]
    lg = lt[N_EXPERTS:N_EXPERTS + SUBLANES]
    iota8 = lax.broadcasted_iota(jnp.int32, (SUBLANES, tr), 0)

    gmax = jnp.max(lg, axis=0, keepdims=True)
    gidx = jnp.min(jnp.where(lg == gmax, iota8, SUBLANES), axis=0, keepdims=True)
    gw = 1.0 / jnp.sum(jnp.exp(lg - gmax), axis=0, keepdims=True)

    ein = le[0:EXPERTS_PER_GROUP]
    for g in range(1, N_GROUPS):
        ein = jnp.where(gidx == g, le[g * EXPERTS_PER_GROUP:(g + 1) * EXPERTS_PER_GROUP], ein)
    ee = jnp.exp(ein - jnp.max(ein, axis=0, keepdims=True))
    prob = ee / jnp.sum(ee, axis=0, keepdims=True)
    p1 = jnp.max(prob, axis=0, keepdims=True)
    i1 = jnp.min(jnp.where(prob == p1, iota8, SUBLANES), axis=0, keepdims=True)
    rest = jnp.where(iota8 == i1, -1.0, prob)
    p2 = jnp.max(rest, axis=0, keepdims=True)
    i2 = jnp.min(jnp.where(rest == p2, iota8, SUBLANES), axis=0, keepdims=True)
    den = p1 + p2
    w1 = gw * (p1 / den)
    w2 = gw * (p2 / den)
    first_lo = i1 < i2
    lo = jnp.minimum(i1, i2)
    hi = jnp.maximum(i1, i2)
    w_lo = jnp.where(first_lo, w1, w2)
    w_hi = jnp.where(first_lo, w2, w1)
    bkt = gidx * (EXPERTS_PER_GROUP * EXPERTS_PER_GROUP) + lo * EXPERTS_PER_GROUP + hi
    bkt_ref[...] = bkt

    onehot = lax.broadcasted_iota(jnp.int32, (N_BUCKETS, tr), 0) == bkt
    ob = jnp.where(onehot, 1.0, 0.0).astype(jnp.bfloat16)
    before = _dot(ob, u_ref[...])
    within = jnp.sum(jnp.where(onehot, before, 0.0), axis=0, keepdims=True)
    carry = carry_ref[...]
    c_hi = jnp.floor(carry * (1.0 / 256.0))
    c_lo = carry - 256.0 * c_hi
    prev = 256.0 * _dot(c_hi.astype(jnp.bfloat16), ob) + _dot(c_lo.astype(jnp.bfloat16), ob)
    rank_ref[...] = (within + prev[0:1]).astype(jnp.int32)
    carry_ref[...] = carry + _dot_nt(jnp.ones((SUBLANES, tr), jnp.bfloat16), ob)

    @pl.when(i == pl.num_programs(0) - 1)
    def _():
        cnt_ref[...] = carry_ref[...]

    for c in range(ROW_CHUNKS):
        aug_ref[pl.ds(c, tr, stride=AUG_ROWS), :] = x[:, c * LANES:(c + 1) * LANES]
    aug_ref[pl.ds(ROW_CHUNKS, tr, stride=AUG_ROWS), :] = jnp.broadcast_to(w_lo, (LANES, tr)).T
    aug_ref[pl.ds(ROW_CHUNKS + 1, tr, stride=AUG_ROWS), :] = jnp.broadcast_to(w_hi, (LANES, tr)).T


def _route(x1, wr, br, upper):
    n = x1.shape[0]
    tr = TILE_R
    const = lambda i: (0, 0)
    return pl.pallas_call(
        _route_kernel,
        grid=(n // tr,),
        in_specs=[
            pl.BlockSpec((tr, D_MODEL), lambda i: (i, 0)),
            pl.BlockSpec(wr.shape, const),
            pl.BlockSpec(br.shape, const),
            pl.BlockSpec((tr, tr), const),
        ],
        out_specs=[
            pl.BlockSpec((tr * AUG_ROWS, LANES), lambda i: (i, 0)),
            pl.BlockSpec((1, tr), lambda i: (0, i)),
            pl.BlockSpec((1, tr), lambda i: (0, i)),
            pl.BlockSpec((SUBLANES, N_BUCKETS), const),
        ],
        out_shape=[
            jax.ShapeDtypeStruct((n * AUG_ROWS, LANES), jnp.float32),
            jax.ShapeDtypeStruct((1, n), jnp.int32),
            jax.ShapeDtypeStruct((1, n), jnp.int32),
            jax.ShapeDtypeStruct((SUBLANES, N_BUCKETS), jnp.float32),
        ],
        scratch_shapes=[pltpu.VMEM((SUBLANES, N_BUCKETS), jnp.float32)],
        compiler_params=pltpu.CompilerParams(
            dimension_semantics=("arbitrary",), vmem_limit_bytes=VMEM_LIMIT_BYTES),
        name="route",
    )(x1, wr, br, upper)


def _plan_kernel(bkt_ref, rank_ref, cnt_ref, u_ref, pos_ref, tb_ref, tv_ref, tr_ref, *,
                 n_tiles_pad):
    tm = TILE_M
    cnt = cnt_ref[...]
    ntl = jnp.floor((cnt + (tm - 0.5)) / tm)
    upper = u_ref[...]
    tstart = _dot(ntl.astype(jnp.bfloat16), upper)
    tend = tstart + ntl
    c_hi = jnp.floor(cnt * (1.0 / 256.0))
    c_lo = cnt - 256.0 * c_hi
    bstart = (256.0 * _dot(c_hi.astype(jnp.bfloat16), upper)
              + _dot(c_lo.astype(jnp.bfloat16), upper))

    def pick(table, onehot_bf):
        t_hi = jnp.floor(table * (1.0 / 256.0))
        t_lo = table - 256.0 * t_hi
        return (256.0 * _dot(t_hi.astype(jnp.bfloat16), onehot_bf)
                + _dot(t_lo.astype(jnp.bfloat16), onehot_bf))[0:1]

    bkt = bkt_ref[...]
    onehot = lax.broadcasted_iota(jnp.int32, (N_BUCKETS, bkt.shape[1]), 0) == bkt
    ob = jnp.where(onehot, 1.0, 0.0).astype(jnp.bfloat16)
    pos_ref[...] = pick(bstart, ob).astype(jnp.int32) + rank_ref[...]

    eye = (lax.broadcasted_iota(jnp.int32, (N_BUCKETS, N_BUCKETS), 0)
           == lax.broadcasted_iota(jnp.int32, (N_BUCKETS, N_BUCKETS), 1))
    tend_col = jnp.sum(jnp.where(eye, tend[0:1], 0.0), axis=1, keepdims=True)
    tile = lax.broadcasted_iota(jnp.int32, (1, n_tiles_pad), 1).astype(jnp.float32)
    tbk = jnp.sum(jnp.where(tend_col <= tile, 1.0, 0.0), axis=0, keepdims=True)
    tbk_i = tbk.astype(jnp.int32)
    oh_t = lax.broadcasted_iota(jnp.int32, (N_BUCKETS, n_tiles_pad), 0) == tbk_i
    oh_tb = jnp.where(oh_t, 1.0, 0.0).astype(jnp.bfloat16)
    done = (tile - pick(tstart, oh_tb)) * tm
    valid = jnp.clip(pick(cnt, oh_tb) - done, 0.0, float(tm))
    tv_ref[...] = valid.astype(jnp.int32)
    tr_ref[...] = jnp.where(valid > 0.0, pick(bstart, oh_tb) + done, 0.0).astype(jnp.int32)
    tb_ref[...] = jnp.minimum(tbk_i, N_BUCKETS - 1)


def _plan(bkt, rank, cnt, upper, n_tiles_pad):
    n = bkt.shape[1]
    const = lambda i: (0, 0)
    tile_spec = pl.BlockSpec((1, n_tiles_pad), const)
    tile_shape = jax.ShapeDtypeStruct((1, n_tiles_pad), jnp.int32)
    return pl.pallas_call(
        functools.partial(_plan_kernel, n_tiles_pad=n_tiles_pad),
        grid=(n // TILE_P,),
        in_specs=[
            pl.BlockSpec((1, TILE_P), lambda i: (0, i)),
            pl.BlockSpec((1, TILE_P), lambda i: (0, i)),
            pl.BlockSpec((SUBLANES, N_BUCKETS), const),
            pl.BlockSpec((N_BUCKETS, N_BUCKETS), const),
        ],
        out_specs=[pl.BlockSpec((1, TILE_P), lambda i: (0, i)), tile_spec, tile_spec, tile_spec],
        out_shape=[jax.ShapeDtypeStruct((1, n), jnp.int32), tile_shape, tile_shape, tile_shape],
        compiler_params=pltpu.CompilerParams(dimension_semantics=("arbitrary",)),
        name="plan",
    )(bkt, rank, cnt, upper)


def _dispatch_kernel(pos_ref, x_ref, xs_hbm, zbuf, sem, zsem, *, n_tokens):
    i = pl.program_id(0)
    td = TILE_D

    @pl.when(i == 0)
    def _():
        zbuf[...] = jnp.zeros_like(zbuf)
        tail = pltpu.make_async_copy(
            zbuf, xs_hbm.at[pl.ds(n_tokens * AUG_ROWS, TILE_M * AUG_ROWS), :], zsem.at[0])
        tail.start()
        tail.wait()

    copies = [pltpu.make_async_copy(
        x_ref.at[pl.ds(r * AUG_ROWS, AUG_ROWS), :],
        xs_hbm.at[pl.ds(pos_ref[i * td + r] * AUG_ROWS, AUG_ROWS), :],
        sem.at[0]) for r in range(td)]
    for r, cp in enumerate(copies):
        cp.start(priority=r % 2)
    for cp in copies:
        cp.wait()


def _dispatch(pos, x1t):
    n = pos.shape[0]
    td = TILE_D
    grid_spec = pltpu.PrefetchScalarGridSpec(
        num_scalar_prefetch=1,
        grid=(n // td,),
        in_specs=[pl.BlockSpec((td * AUG_ROWS, LANES), lambda i, pos: (i, 0))],
        out_specs=pl.BlockSpec(memory_space=pl.ANY),
        scratch_shapes=[
            pltpu.VMEM((TILE_M * AUG_ROWS, LANES), jnp.float32),
            pltpu.SemaphoreType.DMA((1,)),
            pltpu.SemaphoreType.DMA((1,)),
        ],
    )
    return pl.pallas_call(
        functools.partial(_dispatch_kernel, n_tokens=n),
        grid_spec=grid_spec,
        out_shape=jax.ShapeDtypeStruct(((n + TILE_M) * AUG_ROWS, LANES), jnp.float32),
        compiler_params=pltpu.CompilerParams(
            dimension_semantics=("arbitrary",), vmem_limit_bytes=VMEM_LIMIT_BYTES),
        name="dispatch",
    )(pos, x1t)


def _moe_kernel(pos_ref, tb_ref, tv_ref, tr_ref, xs_hbm, wgu1_ref, wgu2_ref, wd1_ref, wd2_ref,
                g2_ref, b2_ref, out_hbm, src_ref, xbuf0, xbuf1, obuf0, obuf1, gsem, ssem, *,
                n_tokens, n_tiles):
    i = pl.program_id(0)
    tm = TILE_M
    xbufs = (xbuf0, xbuf1)
    obufs = (obuf0, obuf1)

    def tile_copy(slot, tile):
        return pltpu.make_async_copy(
            xs_hbm.at[pl.ds(tr_ref[tile] * AUG_ROWS, tm * AUG_ROWS), :], xbufs[slot], gsem.at[slot])

    def scatter_copy(slot, dst, r):
        return pltpu.make_async_copy(
            obufs[slot].at[pl.ds(r, 1), :], out_hbm.at[pl.ds(dst, 1), :], ssem.at[slot])

    def for_valid_rows(tile, fn):
        nv = tv_ref[tile]
        for g in range(tm // SCATTER_GROUP):
            @pl.when(nv >= (g + 1) * SCATTER_GROUP)
            def _(g=g):
                for k in range(SCATTER_GROUP):
                    fn(g * SCATTER_GROUP + k, k)
        p = SCATTER_GROUP // 2
        while p:
            @pl.when(jnp.bitwise_and(nv, p) != 0)
            def _(p=p):
                base = jnp.bitwise_and(nv, -2 * p)
                for k in range(p):
                    fn(base + k, k)
            p //= 2

    def start_scatter(slot, tile):
        row0 = tr_ref[tile]
        for_valid_rows(tile, lambda r, k: scatter_copy(slot, src_ref[row0 + r], r)
                       .start(priority=k % 2))

    def wait_scatter(slot, tile):
        for_valid_rows(tile, lambda r, k: scatter_copy(slot, 0, 0).wait())

    @pl.when(i == 0)
    def _():
        def invert(c, _):
            for u in range(8):
                t = c * 8 + u
                src_ref[pos_ref[t]] = t
            return 0
        lax.fori_loop(0, n_tokens // 8, invert, 0)
        tile_copy(0, 0).start()

    def tile_body(slot):
        has_next = jnp.logical_and(i + 1 < n_tiles, tv_ref[jnp.minimum(i + 1, n_tiles - 1)] > 0)
        tile_copy(slot, i).wait()
        tile_copy(1 - slot, jnp.where(has_next, i + 1, i)).start()

        xb_ref = xbufs[slot]
        chunk = lambda c: xb_ref[pl.ds(c, tm, stride=AUG_ROWS), :]
        x = jnp.concatenate([chunk(c) for c in range(ROW_CHUNKS)], axis=1)
        xb = x.astype(jnp.bfloat16)
        acc = jnp.zeros((tm, D_MODEL), jnp.float32)
        for e, (wgu_ref, wd_ref) in enumerate(((wgu1_ref, wd1_ref), (wgu2_ref, wd2_ref))):
            gu = _dot(xb, wgu_ref[...])
            hg = gu[:, 0:EXPERT_HIDDEN]
            hu = gu[:, EXPERT_HIDDEN:]
            hid = (hg * _sigmoid(hg) * hu).astype(jnp.bfloat16)
            y = _dot(hid, wd_ref[...])
            acc = acc + jnp.concatenate([chunk(ROW_CHUNKS + e)] * ROW_CHUNKS, axis=1) * y
        res = _layer_norm(DEEPNORM_ALPHA * x + acc, g2_ref[...], b2_ref[...])

        @pl.when(i >= 2)
        def _():
            wait_scatter(slot, i - 2)

        obufs[slot][...] = res
        start_scatter(slot, i)

        @pl.when(jnp.logical_not(has_next))
        def _():
            tile_copy(1 - slot, i).wait()

            @pl.when(i >= 1)
            def _():
                wait_scatter(1 - slot, i - 1)
            wait_scatter(slot, i)

    used = tv_ref[i] > 0
    for slot in (0, 1):
        @pl.when(jnp.logical_and(used, i % 2 == slot))
        def _(slot=slot):
            tile_body(slot)


def _moe(pos, tbk, tval, trow, xs, wgu_bf, wd_bf, g2, b2, n_tiles):
    n = pos.shape[0]
    tm = TILE_M
    epg = EXPERTS_PER_GROUP
    e_lo = lambda i, pos, tb, tv, tr: (tb[i] // epg, 0, 0)
    e_hi = lambda i, pos, tb, tv, tr: ((tb[i] // (epg * epg)) * epg + tb[i] % epg, 0, 0)
    const = lambda i, pos, tb, tv, tr: (0, 0)
    grid_spec = pltpu.PrefetchScalarGridSpec(
        num_scalar_prefetch=4,
        grid=(n_tiles,),
        in_specs=[
            pl.BlockSpec(memory_space=pl.ANY),
            pl.BlockSpec((None, D_MODEL, 2 * EXPERT_HIDDEN), e_lo),
            pl.BlockSpec((None, D_MODEL, 2 * EXPERT_HIDDEN), e_hi),
            pl.BlockSpec((None, EXPERT_HIDDEN, D_MODEL), e_lo),
            pl.BlockSpec((None, EXPERT_HIDDEN, D_MODEL), e_hi),
            pl.BlockSpec((1, D_MODEL), const),
            pl.BlockSpec((1, D_MODEL), const),
        ],
        out_specs=pl.BlockSpec(memory_space=pl.ANY),
        scratch_shapes=[
            pltpu.SMEM((n,), jnp.int32),
            pltpu.VMEM((tm * AUG_ROWS, LANES), jnp.float32),
            pltpu.VMEM((tm * AUG_ROWS, LANES), jnp.float32),
            pltpu.VMEM((tm, D_MODEL), jnp.float32),
            pltpu.VMEM((tm, D_MODEL), jnp.float32),
            pltpu.SemaphoreType.DMA((2,)),
            pltpu.SemaphoreType.DMA((2,)),
        ],
    )
    return pl.pallas_call(
        functools.partial(_moe_kernel, n_tokens=n, n_tiles=n_tiles),
        grid_spec=grid_spec,
        out_shape=jax.ShapeDtypeStruct((n, D_MODEL), jnp.float32),
        compiler_params=pltpu.CompilerParams(
            dimension_semantics=("arbitrary",), vmem_limit_bytes=VMEM_LIMIT_BYTES),
        name="moe",
    )(pos, tbk, tval, trow, xs, wgu_bf, wgu_bf, wd_bf, wd_bf, g2, b2)


def _attention_tables(attn_sink):
    slopes = jnp.exp2(-8.0 * jnp.arange(1, N_Q_HEADS + 1, dtype=jnp.float32) / N_Q_HEADS)
    qi = jnp.arange(128)[:, None]
    kj = jnp.arange(3 * 128)[None, :]
    rel = kj - 128 - qi
    dist = jnp.abs(rel).astype(jnp.float32)
    band = jnp.abs(rel) <= WINDOW
    bias = jnp.where(band[None], -slopes[:, None, None] * dist[None], NEG_INF)
    bias = bias.reshape(N_KV_HEADS, 2 * 128, 3 * 128)
    sink = jnp.broadcast_to(attn_sink.astype(jnp.float32)[:, None], (N_Q_HEADS, 128))
    return bias, sink.reshape(N_KV_HEADS, 2 * 128, 1)


def kernel(x, w_in, b_gate, w_dw, b_dw, conv_ln_g, conv_ln_b, w_conv_proj, attn_sink, w_attn_proj,
           w_out, ln1_g, ln1_b, w_router_group, b_router_group, w_router_expert, b_router_expert,
           w_gate_up, w_down, ln2_g, ln2_b):
    nb, s, d = x.shape
    n = nb * s
    assert d == D_MODEL and s % TILE_B == 0 and n % TILE_A == 0 and n % TILE_R == 0
    assert n % TILE_P == 0 and n % TILE_D == 0
    bf = jnp.bfloat16
    l = 0
    row = lambda a: a[l].reshape(1, -1)

    h, q, k, v, gt = _inproj(x.reshape(n, d), w_in[l].astype(bf), row(b_gate))

    wdw8 = jnp.repeat(w_dw[l], SUBLANES, axis=0)
    bias_tab, sink_tab = _attention_tables(attn_sink[l])
    r3 = lambda a: a.reshape(nb, s, a.shape[-1])
    x1 = _mix(r3(h), r3(q), r3(k), r3(v), r3(gt), x, wdw8, row(b_dw), row(conv_ln_g),
              row(conv_ln_b), w_conv_proj[l].astype(bf), bias_tab, sink_tab,
              w_attn_proj[l].astype(bf), w_out[l].astype(bf), row(ln1_g), row(ln1_b))

    pad = SUBLANES - N_GROUPS
    wr = jnp.concatenate([w_router_expert[l].T, w_router_group[l].T,
                          jnp.zeros((pad, d), jnp.float32)], axis=0)
    br = jnp.concatenate([b_router_expert[l], b_router_group[l],
                          jnp.full((pad,), NEG_INF, jnp.float32)]).reshape(-1, 1)
    upper_r = jnp.triu(jnp.ones((TILE_R, TILE_R), bf), k=1)
    x1t, bkt, rank, cnt = _route(x1.reshape(n, d), wr, br, upper_r)

    n_tiles = -(-n // TILE_M) + N_PAIRS
    n_tiles_pad = -(-n_tiles // LANES) * LANES
    upper_b = jnp.triu(jnp.ones((N_BUCKETS, N_BUCKETS), bf), k=1)
    pos, tbk, tval, trow = _plan(bkt, rank, cnt, upper_b, n_tiles_pad)
    pos = pos.reshape(n)

    xs = _dispatch(pos, x1t)
    out = _moe(pos, tbk.reshape(-1), tval.reshape(-1), trow.reshape(-1), xs,
               w_gate_up[l].astype(bf), w_down[l].astype(bf), row(ln2_g), row(ln2_b), n_tiles)
    return out.reshape(nb, s, d)
```

```python
import functools

import jax
import jax.numpy as jnp
from jax import lax
from jax.experimental import pallas as pl
from jax.experimental.pallas import tpu as pltpu

D_MODEL = 1024
N_Q_HEADS = 8
N_KV_HEADS = 4
HEAD_DIM = 128
WINDOW = 128
NEG_INF = -1e30
CONV_WIDTH = 31
CONV_PAD = (CONV_WIDTH - 1) // 2
Q_W = N_Q_HEADS * HEAD_DIM
KV_W = N_KV_HEADS * HEAD_DIM
Q_OFF = 2 * D_MODEL
K_OFF = Q_OFF + Q_W
V_OFF = K_OFF + KV_W
G_OFF = V_OFF + KV_W
IN_WIDTH = G_OFF + 2 * D_MODEL
N_GROUPS = 4
EXPERTS_PER_GROUP = 8
N_EXPERTS = N_GROUPS * EXPERTS_PER_GROUP
EXPERT_HIDDEN = D_MODEL // 4
LN_EPS = 1e-5
DEPTH = 1
DEEPNORM_ALPHA = (2.0 * DEPTH) ** 0.25

LANES = 128
SUBLANES = 8
VMEM_LIMIT_BYTES = 56 * 1024 * 1024

TILE_A = 512
TILE_B = 256
CONV_HALO = 16
CONV_ROWS = 128
ATTN_STREAMS = 8
CONV_STEPS_PER_TURN = 4
TILE_R = 512
TILE_P = 2048
TILE_D = 512
TILE_M = 320
SCATTER_GROUP = 32
N_BUCKETS = N_GROUPS * EXPERTS_PER_GROUP * EXPERTS_PER_GROUP
N_PAIRS = N_GROUPS * (EXPERTS_PER_GROUP * (EXPERTS_PER_GROUP - 1) // 2)
ROW_CHUNKS = D_MODEL // LANES
AUG_ROWS = ROW_CHUNKS + 2
_NT = (((1,), (1,)), ((), ()))


def _dot(a, b):
    return jnp.dot(a, b, preferred_element_type=jnp.float32)


def _dot_nt(a, b):
    return lax.dot_general(a, b, _NT, preferred_element_type=jnp.float32)


def _sigmoid(x):
    return 1.0 / (1.0 + jnp.exp(-x))


def _layer_norm(y, g, b):
    mu = jnp.mean(y, axis=-1, keepdims=True)
    d = y - mu
    var = jnp.mean(d * d, axis=-1, keepdims=True)
    return d * lax.rsqrt(var + LN_EPS) * g + b


def _split_bf16(x):
    hi = x.astype(jnp.bfloat16)
    lo = (x - hi.astype(jnp.float32)).astype(jnp.bfloat16)
    return hi, lo


def _inproj_kernel(x_ref, w_ref, bg_ref, h_ref, q_ref, k_ref, v_ref, gt_ref):
    xb = x_ref[...].astype(jnp.bfloat16)
    a = _dot(xb, w_ref[:, 0:D_MODEL])
    g = _dot(xb, w_ref[:, D_MODEL:Q_OFF])
    h_ref[...] = a * _sigmoid(g)
    q = _dot(xb, w_ref[:, Q_OFF:K_OFF]) * (HEAD_DIM ** -0.5)
    q_ref[...] = q.astype(jnp.bfloat16)
    k_ref[...] = _dot(xb, w_ref[:, K_OFF:V_OFF]).astype(jnp.bfloat16)
    v_ref[...] = _dot(xb, w_ref[:, V_OFF:G_OFF]).astype(jnp.bfloat16)
    gt = _dot(xb, w_ref[:, G_OFF:IN_WIDTH]) + bg_ref[...]
    gt_ref[...] = _sigmoid(gt)


def _inproj(x2, w_in_bf, b_gate):
    n = x2.shape[0]
    const = lambda i: (0, 0)
    row = lambda i: (i, 0)
    return pl.pallas_call(
        _inproj_kernel,
        grid=(n // TILE_A,),
        in_specs=[
            pl.BlockSpec((TILE_A, D_MODEL), row),
            pl.BlockSpec((D_MODEL, IN_WIDTH), const, pipeline_mode=pl.Buffered(1)),
            pl.BlockSpec((1, 2 * D_MODEL), const),
        ],
        out_specs=[
            pl.BlockSpec((TILE_A, D_MODEL), row),
            pl.BlockSpec((TILE_A, Q_W), row),
            pl.BlockSpec((TILE_A, KV_W), row),
            pl.BlockSpec((TILE_A, KV_W), row),
            pl.BlockSpec((TILE_A, 2 * D_MODEL), row),
        ],
        out_shape=[
            jax.ShapeDtypeStruct((n, D_MODEL), jnp.float32),
            jax.ShapeDtypeStruct((n, Q_W), jnp.bfloat16),
            jax.ShapeDtypeStruct((n, KV_W), jnp.bfloat16),
            jax.ShapeDtypeStruct((n, KV_W), jnp.bfloat16),
            jax.ShapeDtypeStruct((n, 2 * D_MODEL), jnp.float32),
        ],
        compiler_params=pltpu.CompilerParams(
            dimension_semantics=("arbitrary",), vmem_limit_bytes=VMEM_LIMIT_BYTES),
        name="inproj",
    )(x2, w_in_bf, b_gate)


def _conv_block(hs_ref, wdw_ref, cv_ref, row0, lane0, rows):
    ng = rows // SUBLANES
    nw = ng + 2 * CONV_HALO // SUBLANES
    lanes = slice(lane0, lane0 + LANES)
    win = [hs_ref[row0 + SUBLANES * g: row0 + SUBLANES * (g + 1), lanes] for g in range(nw)]
    sub = lax.broadcasted_iota(jnp.int32, (SUBLANES, LANES), 0)
    acc = [None] * ng
    for s in range(1, SUBLANES + 1):
        if s == SUBLANES:
            shifted = win[1:]
        else:
            rolled = [pltpu.roll(w, SUBLANES - s, axis=0) for w in win]
            keep = sub < (SUBLANES - s)
            shifted = [jnp.where(keep, rolled[g], rolled[g + 1]) for g in range(nw - 1)]
        for a in range((CONV_WIDTH + SUBLANES - 1) // SUBLANES):
            j = s - 1 + SUBLANES * a
            if j >= CONV_WIDTH:
                continue
            wv = wdw_ref[SUBLANES * j: SUBLANES * (j + 1), lanes]
            for g in range(ng):
                term = wv * shifted[g + a]
                acc[g] = term if acc[g] is None else acc[g] + term
        if s == SUBLANES:
            for g, v in enumerate(acc):
                cv_ref[row0 + SUBLANES * g: row0 + SUBLANES * (g + 1), lanes] = v
        yield


def _chain(gens):
    for g in gens:
        yield from g


def _round_robin(streams):
    live = [[g, n] for g, n in streams]
    while live:
        for item in list(live):
            for _ in range(item[1]):
                try:
                    next(item[0])
                except StopIteration:
                    live.remove(item)
                    break


def _mix_kernel(hl_ref, hc_ref, hr_ref, q_ref, kl_ref, kc_ref, kr_ref, vl_ref, vc_ref, vr_ref,
                gt_ref, x_ref, wdw_ref, bdw_ref, cg_ref, cb_ref, wcp_ref, bias_ref, sink_ref,
                wap_ref, wo_ref, g1_ref, b1_ref, x1_ref, hs_ref, cv_ref, at_ref):
    i = pl.program_id(1)
    first = i == 0
    last = i == pl.num_programs(1) - 1
    tb = TILE_B

    hs_ref[0:CONV_HALO, :] = jnp.where(first, 0.0, hl_ref[...])
    hs_ref[CONV_HALO:CONV_HALO + tb, :] = hc_ref[...]
    hs_ref[CONV_HALO + tb:, :] = jnp.where(last, 0.0, hr_ref[...])
    conv_stream = _chain(_conv_block(hs_ref, wdw_ref, cv_ref, r0, lc * LANES, CONV_ROWS)
                         for r0 in range(0, tb, CONV_ROWS) for lc in range(D_MODEL // LANES))

    nq = tb // 128
    neg_l = jnp.where(first, NEG_INF, 0.0)
    neg_r = jnp.where(last, NEG_INF, 0.0)
    lane = lax.broadcasted_iota(jnp.int32, (1, 3 * 128), 1)
    edge_l = jnp.where(lane < 128, neg_l, 0.0)
    edge_r = jnp.where(lane >= 256, neg_r, 0.0)

    def attn_task(j, g):
        rs = slice(j * 128, (j + 1) * 128)
        cs = slice(g * HEAD_DIM, (g + 1) * HEAD_DIM)
        if j == 0:
            k0, v0 = kl_ref[:, cs], vl_ref[:, cs]
        else:
            k0, v0 = kc_ref[(j - 1) * 128:j * 128, cs], vc_ref[(j - 1) * 128:j * 128, cs]
        if j == nq - 1:
            k2, v2 = kr_ref[:, cs], vr_ref[:, cs]
        else:
            k2, v2 = kc_ref[(j + 1) * 128:(j + 2) * 128, cs], vc_ref[(j + 1) * 128:(j + 2) * 128, cs]
        k3 = jnp.concatenate([k0, kc_ref[rs, cs], k2], axis=0)
        v3 = jnp.concatenate([v0, vc_ref[rs, cs], v2], axis=0)
        h0 = 2 * g
        q2 = jnp.concatenate([q_ref[rs, h0 * HEAD_DIM:(h0 + 1) * HEAD_DIM],
                              q_ref[rs, (h0 + 1) * HEAD_DIM:(h0 + 2) * HEAD_DIM]], axis=0)
        s = _dot_nt(q2, k3) + bias_ref[g]
        if j == 0:
            s = s + edge_l
        if j == nq - 1:
            s = s + edge_r
        yield
        sk = sink_ref[g]
        m = jnp.maximum(jnp.max(s, axis=1, keepdims=True), sk)
        p = jnp.exp(s - m)
        yield
        den = jnp.sum(p, axis=1, keepdims=True) + jnp.exp(sk - m)
        o = _dot(p.astype(jnp.bfloat16), v3)
        yield
        o = o / den
        at_ref[rs, h0 * HEAD_DIM:(h0 + 1) * HEAD_DIM] = o[0:128].astype(jnp.bfloat16)
        at_ref[rs, (h0 + 1) * HEAD_DIM:(h0 + 2) * HEAD_DIM] = o[128:256].astype(jnp.bfloat16)
        yield

    attn_tasks = [(j, g) for j in range(nq) for g in range(N_KV_HEADS)]
    streams = []
    for k in range(ATTN_STREAMS):
        streams += [(_chain(attn_task(j, g) for j, g in attn_tasks[k::ATTN_STREAMS]), 1),
                    (conv_stream, CONV_STEPS_PER_TURN)]
    _round_robin(streams)

    c = _layer_norm(cv_ref[...] + bdw_ref[...], cg_ref[...], cb_ref[...])
    c = (c * _sigmoid(c)).astype(jnp.bfloat16)
    conv_out = _dot(c, wcp_ref[...])
    attn_out = _dot(at_ref[...], wap_ref[...])

    merged = gt_ref[:, 0:D_MODEL] * conv_out + gt_ref[:, D_MODEL:] * attn_out
    mixed = _dot(merged.astype(jnp.bfloat16), wo_ref[...])
    x1_ref[...] = _layer_norm(DEEPNORM_ALPHA * x_ref[...] + mixed, g1_ref[...], b1_ref[...])


def _mix(h3, q3, k3, v3, gt3, x3, wdw8, b_dw, cg, cb, wcp, bias_tab, sink_tab, wap, wo, g1, b1):
    nb, s, _ = x3.shape
    tb = TILE_B
    nt = s // tb
    hb = tb // CONV_HALO
    kb = tb // 128
    const2 = lambda b, i: (0, 0)
    const3 = lambda b, i: (0, 0, 0)
    cur = lambda b, i: (b, i, 0)
    w_spec = lambda shape: pl.BlockSpec(shape, const2, pipeline_mode=pl.Buffered(1))
    in_specs = [
        pl.BlockSpec((None, CONV_HALO, D_MODEL), lambda b, i: (b, jnp.maximum(i * hb - 1, 0), 0)),
        pl.BlockSpec((None, tb, D_MODEL), cur),
        pl.BlockSpec((None, CONV_HALO, D_MODEL),
                     lambda b, i: (b, jnp.minimum((i + 1) * hb, s // CONV_HALO - 1), 0)),
        pl.BlockSpec((None, tb, Q_W), cur),
    ]
    for _ in range(2):
        in_specs += [
            pl.BlockSpec((None, 128, KV_W), lambda b, i: (b, jnp.maximum(i * kb - 1, 0), 0)),
            pl.BlockSpec((None, tb, KV_W), cur),
            pl.BlockSpec((None, 128, KV_W), lambda b, i: (b, jnp.minimum((i + 1) * kb, s // 128 - 1), 0)),
        ]
    in_specs += [
        pl.BlockSpec((None, tb, 2 * D_MODEL), cur),
        pl.BlockSpec((None, tb, D_MODEL), cur),
        w_spec((CONV_WIDTH * SUBLANES, D_MODEL)),
        pl.BlockSpec((1, D_MODEL), const2),
        pl.BlockSpec((1, D_MODEL), const2),
        pl.BlockSpec((1, D_MODEL), const2),
        w_spec((D_MODEL, D_MODEL)),
        pl.BlockSpec((N_KV_HEADS, 256, 3 * 128), const3, pipeline_mode=pl.Buffered(1)),
        pl.BlockSpec((N_KV_HEADS, 256, 1), const3),
        w_spec((Q_W, D_MODEL)),
        w_spec((D_MODEL, D_MODEL)),
        pl.BlockSpec((1, D_MODEL), const2),
        pl.BlockSpec((1, D_MODEL), const2),
    ]
    return pl.pallas_call(
        _mix_kernel,
        grid=(nb, nt),
        in_specs=in_specs,
        out_specs=pl.BlockSpec((None, tb, D_MODEL), cur),
        out_shape=jax.ShapeDtypeStruct((nb, s, D_MODEL), jnp.float32),
        scratch_shapes=[
            pltpu.VMEM((tb + 2 * CONV_HALO, D_MODEL), jnp.float32),
            pltpu.VMEM((tb, D_MODEL), jnp.float32),
            pltpu.VMEM((tb, Q_W), jnp.bfloat16),
        ],
        compiler_params=pltpu.CompilerParams(
            dimension_semantics=("arbitrary", "arbitrary"), vmem_limit_bytes=VMEM_LIMIT_BYTES),
        name="mix",
    )(h3, h3, h3, q3, k3, k3, k3, v3, v3, v3, gt3, x3, wdw8, b_dw, cg, cb, wcp, bias_tab,
      sink_tab, wap, wo, g1, b1)


def _route_kernel(x_ref, wr_ref, br_ref, u_ref, aug_ref, bkt_ref, rank_ref, cnt_ref, carry_ref):
    i = pl.program_id(0)
    tr = TILE_R

    @pl.when(i == 0)
    def _():
        carry_ref[...] = jnp.zeros_like(carry_ref)

    x = x_ref[...]
    xh, xl = _split_bf16(x)
    wh, wl = _split_bf16(wr_ref[...])
    lt = _dot_nt(wh, xh) + _dot_nt(wh, xl) + _dot_nt(wl, xh) + br_ref[...]
    le = lt[0:N_EXPERTS]
    lg = lt[N_EXPERTS:N_EXPERTS + SUBLANES]
    iota8 = lax.broadcasted_iota(jnp.int32, (SUBLANES, tr), 0)

    gmax = jnp.max(lg, axis=0, keepdims=True)
    gidx = jnp.min(jnp.where(lg == gmax, iota8, SUBLANES), axis=0, keepdims=True)
    gw = 1.0 / jnp.sum(jnp.exp(lg - gmax), axis=0, keepdims=True)

    ein = le[0:EXPERTS_PER_GROUP]
    for g in range(1, N_GROUPS):
        ein = jnp.where(gidx == g, le[g * EXPERTS_PER_GROUP:(g + 1) * EXPERTS_PER_GROUP], ein)
    ee = jnp.exp(ein - jnp.max(ein, axis=0, keepdims=True))
    prob = ee / jnp.sum(ee, axis=0, keepdims=True)
    p1 = jnp.max(prob, axis=0, keepdims=True)
    i1 = jnp.min(jnp.where(prob == p1, iota8, SUBLANES), axis=0, keepdims=True)
    rest = jnp.where(iota8 == i1, -1.0, prob)
    p2 = jnp.max(rest, axis=0, keepdims=True)
    i2 = jnp.min(jnp.where(rest == p2, iota8, SUBLANES), axis=0, keepdims=True)
    den = p1 + p2
    w1 = gw * (p1 / den)
    w2 = gw * (p2 / den)
    first_lo = i1 < i2
    lo = jnp.minimum(i1, i2)
    hi = jnp.maximum(i1, i2)
    w_lo = jnp.where(first_lo, w1, w2)
    w_hi = jnp.where(first_lo, w2, w1)
    bkt = gidx * (EXPERTS_PER_GROUP * EXPERTS_PER_GROUP) + lo * EXPERTS_PER_GROUP + hi
    bkt_ref[...] = bkt

    onehot = lax.broadcasted_iota(jnp.int32, (N_BUCKETS, tr), 0) == bkt
    ob = jnp.where(onehot, 1.0, 0.0).astype(jnp.bfloat16)
    before = _dot(ob, u_ref[...])
    within = jnp.sum(jnp.where(onehot, before, 0.0), axis=0, keepdims=True)
    carry = carry_ref[...]
    c_hi = jnp.floor(carry * (1.0 / 256.0))
    c_lo = carry - 256.0 * c_hi
    prev = 256.0 * _dot(c_hi.astype(jnp.bfloat16), ob) + _dot(c_lo.astype(jnp.bfloat16), ob)
    rank_ref[...] = (within + prev[0:1]).astype(jnp.int32)
    carry_ref[...] = carry + _dot_nt(jnp.ones((SUBLANES, tr), jnp.bfloat16), ob)

    @pl.when(i == pl.num_programs(0) - 1)
    def _():
        cnt_ref[...] = carry_ref[...]

    for c in range(ROW_CHUNKS):
        aug_ref[pl.ds(c, tr, stride=AUG_ROWS), :] = x[:, c * LANES:(c + 1) * LANES]
    aug_ref[pl.ds(ROW_CHUNKS, tr, stride=AUG_ROWS), :] = jnp.broadcast_to(w_lo, (LANES, tr)).T
    aug_ref[pl.ds(ROW_CHUNKS + 1, tr, stride=AUG_ROWS), :] = jnp.broadcast_to(w_hi, (LANES, tr)).T


def _route(x1, wr, br, upper):
    n = x1.shape[0]
    tr = TILE_R
    const = lambda i: (0, 0)
    return pl.pallas_call(
        _route_kernel,
        grid=(n // tr,),
        in_specs=[
            pl.BlockSpec((tr, D_MODEL), lambda i: (i, 0)),
            pl.BlockSpec(wr.shape, const),
            pl.BlockSpec(br.shape, const),
            pl.BlockSpec((tr, tr), const),
        ],
        out_specs=[
            pl.BlockSpec((tr * AUG_ROWS, LANES), lambda i: (i, 0)),
            pl.BlockSpec((1, tr), lambda i: (0, i)),
            pl.BlockSpec((1, tr), lambda i: (0, i)),
            pl.BlockSpec((SUBLANES, N_BUCKETS), const),
        ],
        out_shape=[
            jax.ShapeDtypeStruct((n * AUG_ROWS, LANES), jnp.float32),
            jax.ShapeDtypeStruct((1, n), jnp.int32),
            jax.ShapeDtypeStruct((1, n), jnp.int32),
            jax.ShapeDtypeStruct((SUBLANES, N_BUCKETS), jnp.float32),
        ],
        scratch_shapes=[pltpu.VMEM((SUBLANES, N_BUCKETS), jnp.float32)],
        compiler_params=pltpu.CompilerParams(
            dimension_semantics=("arbitrary",), vmem_limit_bytes=VMEM_LIMIT_BYTES),
        name="route",
    )(x1, wr, br, upper)


def _plan_kernel(bkt_ref, rank_ref, cnt_ref, u_ref, pos_ref, tb_ref, tv_ref, tr_ref, *,
                 n_tiles_pad):
    tm = TILE_M
    cnt = cnt_ref[...]
    ntl = jnp.floor((cnt + (tm - 0.5)) / tm)
    upper = u_ref[...]
    tstart = _dot(ntl.astype(jnp.bfloat16), upper)
    tend = tstart + ntl
    c_hi = jnp.floor(cnt * (1.0 / 256.0))
    c_lo = cnt - 256.0 * c_hi
    bstart = (256.0 * _dot(c_hi.astype(jnp.bfloat16), upper)
              + _dot(c_lo.astype(jnp.bfloat16), upper))

    def pick(table, onehot_bf):
        t_hi = jnp.floor(table * (1.0 / 256.0))
        t_lo = table - 256.0 * t_hi
        return (256.0 * _dot(t_hi.astype(jnp.bfloat16), onehot_bf)
                + _dot(t_lo.astype(jnp.bfloat16), onehot_bf))[0:1]

    bkt = bkt_ref[...]
    onehot = lax.broadcasted_iota(jnp.int32, (N_BUCKETS, bkt.shape[1]), 0) == bkt
    ob = jnp.where(onehot, 1.0, 0.0).astype(jnp.bfloat16)
    pos_ref[...] = pick(bstart, ob).astype(jnp.int32) + rank_ref[...]

    eye = (lax.broadcasted_iota(jnp.int32, (N_BUCKETS, N_BUCKETS), 0)
           == lax.broadcasted_iota(jnp.int32, (N_BUCKETS, N_BUCKETS), 1))
    tend_col = jnp.sum(jnp.where(eye, tend[0:1], 0.0), axis=1, keepdims=True)
    tile = lax.broadcasted_iota(jnp.int32, (1, n_tiles_pad), 1).astype(jnp.float32)
    tbk = jnp.sum(jnp.where(tend_col <= tile, 1.0, 0.0), axis=0, keepdims=True)
    tbk_i = tbk.astype(jnp.int32)
    oh_t = lax.broadcasted_iota(jnp.int32, (N_BUCKETS, n_tiles_pad), 0) == tbk_i
    oh_tb = jnp.where(oh_t, 1.0, 0.0).astype(jnp.bfloat16)
    done = (tile - pick(tstart, oh_tb)) * tm
    valid = jnp.clip(pick(cnt, oh_tb) - done, 0.0, float(tm))
    tv_ref[...] = valid.astype(jnp.int32)
    tr_ref[...] = jnp.where(valid > 0.0, pick(bstart, oh_tb) + done, 0.0).astype(jnp.int32)
    tb_ref[...] = jnp.minimum(tbk_i, N_BUCKETS - 1)


def _plan(bkt, rank, cnt, upper, n_tiles_pad):
    n = bkt.shape[1]
    const = lambda i: (0, 0)
    tile_spec = pl.BlockSpec((1, n_tiles_pad), const)
    tile_shape = jax.ShapeDtypeStruct((1, n_tiles_pad), jnp.int32)
    return pl.pallas_call(
        functools.partial(_plan_kernel, n_tiles_pad=n_tiles_pad),
        grid=(n // TILE_P,),
        in_specs=[
            pl.BlockSpec((1, TILE_P), lambda i: (0, i)),
            pl.BlockSpec((1, TILE_P), lambda i: (0, i)),
            pl.BlockSpec((SUBLANES, N_BUCKETS), const),
            pl.BlockSpec((N_BUCKETS, N_BUCKETS), const),
        ],
        out_specs=[pl.BlockSpec((1, TILE_P), lambda i: (0, i)), tile_spec, tile_spec, tile_spec],
        out_shape=[jax.ShapeDtypeStruct((1, n), jnp.int32), tile_shape, tile_shape, tile_shape],
        compiler_params=pltpu.CompilerParams(dimension_semantics=("arbitrary",)),
        name="plan",
    )(bkt, rank, cnt, upper)


def _dispatch_kernel(pos_ref, x_ref, xs_hbm, zbuf, sem, zsem, *, n_tokens):
    i = pl.program_id(0)
    td = TILE_D

    @pl.when(i == 0)
    def _():
        zbuf[...] = jnp.zeros_like(zbuf)
        tail = pltpu.make_async_copy(
            zbuf, xs_hbm.at[pl.ds(n_tokens * AUG_ROWS, TILE_M * AUG_ROWS), :], zsem.at[0])
        tail.start()
        tail.wait()

    copies = [pltpu.make_async_copy(
        x_ref.at[pl.ds(r * AUG_ROWS, AUG_ROWS), :],
        xs_hbm.at[pl.ds(pos_ref[i * td + r] * AUG_ROWS, AUG_ROWS), :],
        sem.at[0]) for r in range(td)]
    for r, cp in enumerate(copies):
        cp.start(priority=r % 2)
    for cp in copies:
        cp.wait()


def _dispatch(pos, x1t):
    n = pos.shape[0]
    td = TILE_D
    grid_spec = pltpu.PrefetchScalarGridSpec(
        num_scalar_prefetch=1,
        grid=(n // td,),
        in_specs=[pl.BlockSpec((td * AUG_ROWS, LANES), lambda i, pos: (i, 0))],
        out_specs=pl.BlockSpec(memory_space=pl.ANY),
        scratch_shapes=[
            pltpu.VMEM((TILE_M * AUG_ROWS, LANES), jnp.float32),
            pltpu.SemaphoreType.DMA((1,)),
            pltpu.SemaphoreType.DMA((1,)),
        ],
    )
    return pl.pallas_call(
        functools.partial(_dispatch_kernel, n_tokens=n),
        grid_spec=grid_spec,
        out_shape=jax.ShapeDtypeStruct(((n + TILE_M) * AUG_ROWS, LANES), jnp.float32),
        compiler_params=pltpu.CompilerParams(
            dimension_semantics=("arbitrary",), vmem_limit_bytes=VMEM_LIMIT_BYTES),
        name="dispatch",
    )(pos, x1t)


def _moe_kernel(pos_ref, tb_ref, tv_ref, tr_ref, xs_hbm, wgu1_ref, wgu2_ref, wd1_ref, wd2_ref,
                g2_ref, b2_ref, out_hbm, src_ref, xbuf0, xbuf1, obuf0, obuf1, gsem, ssem, *,
                n_tokens, n_tiles):
    i = pl.program_id(0)
    tm = TILE_M
    xbufs = (xbuf0, xbuf1)
    obufs = (obuf0, obuf1)

    def tile_copy(slot, tile):
        return pltpu.make_async_copy(
            xs_hbm.at[pl.ds(tr_ref[tile] * AUG_ROWS, tm * AUG_ROWS), :], xbufs[slot], gsem.at[slot])

    def scatter_copy(slot, dst, r):
        return pltpu.make_async_copy(
            obufs[slot].at[pl.ds(r, 1), :], out_hbm.at[pl.ds(dst, 1), :], ssem.at[slot])

    def for_valid_rows(tile, fn):
        nv = tv_ref[tile]
        for g in range(tm // SCATTER_GROUP):
            @pl.when(nv >= (g + 1) * SCATTER_GROUP)
            def _(g=g):
                for k in range(SCATTER_GROUP):
                    fn(g * SCATTER_GROUP + k, k)
        p = SCATTER_GROUP // 2
        while p:
            @pl.when(jnp.bitwise_and(nv, p) != 0)
            def _(p=p):
                base = jnp.bitwise_and(nv, -2 * p)
                for k in range(p):
                    fn(base + k, k)
            p //= 2

    def start_scatter(slot, tile):
        row0 = tr_ref[tile]
        for_valid_rows(tile, lambda r, k: scatter_copy(slot, src_ref[row0 + r], r)
                       .start(priority=k % 2))

    def wait_scatter(slot, tile):
        for_valid_rows(tile, lambda r, k: scatter_copy(slot, 0, 0).wait())

    @pl.when(i == 0)
    def _():
        def invert(c, _):
            for u in range(8):
                t = c * 8 + u
                src_ref[pos_ref[t]] = t
            return 0
        lax.fori_loop(0, n_tokens // 8, invert, 0)
        tile_copy(0, 0).start()

    def tile_body(slot):
        has_next = jnp.logical_and(i + 1 < n_tiles, tv_ref[jnp.minimum(i + 1, n_tiles - 1)] > 0)
        tile_copy(slot, i).wait()
        tile_copy(1 - slot, jnp.where(has_next, i + 1, i)).start()

        xb_ref = xbufs[slot]
        chunk = lambda c: xb_ref[pl.ds(c, tm, stride=AUG_ROWS), :]
        x = jnp.concatenate([chunk(c) for c in range(ROW_CHUNKS)], axis=1)
        xb = x.astype(jnp.bfloat16)
        outs = [None, None]

        def expert(e, wgu_ref, wd_ref):
            gu = _dot(xb, wgu_ref[...])
            yield
            hg = gu[:, 0:EXPERT_HIDDEN]
            hu = gu[:, EXPERT_HIDDEN:]
            hid = (hg * _sigmoid(hg) * hu).astype(jnp.bfloat16)
            yield
            y = _dot(hid, wd_ref[...])
            yield
            outs[e] = jnp.concatenate([chunk(ROW_CHUNKS + e)] * ROW_CHUNKS, axis=1) * y
            yield

        _round_robin([(expert(0, wgu1_ref, wd1_ref), 1), (expert(1, wgu2_ref, wd2_ref), 1)])
        res = _layer_norm(DEEPNORM_ALPHA * x + (outs[0] + outs[1]), g2_ref[...], b2_ref[...])

        @pl.when(i >= 2)
        def _():
            wait_scatter(slot, i - 2)

        obufs[slot][...] = res
        start_scatter(slot, i)

        @pl.when(jnp.logical_not(has_next))
        def _():
            tile_copy(1 - slot, i).wait()

            @pl.when(i >= 1)
            def _():
                wait_scatter(1 - slot, i - 1)
            wait_scatter(slot, i)

    used = tv_ref[i] > 0
    for slot in (0, 1):
        @pl.when(jnp.logical_and(used, i % 2 == slot))
        def _(slot=slot):
            tile_body(slot)


def _moe(pos, tbk, tval, trow, xs, wgu_bf, wd_bf, g2, b2, n_tiles):
    n = pos.shape[0]
    tm = TILE_M
    epg = EXPERTS_PER_GROUP
    e_lo = lambda i, pos, tb, tv, tr: (tb[i] // epg, 0, 0)
    e_hi = lambda i, pos, tb, tv, tr: ((tb[i] // (epg * epg)) * epg + tb[i] % epg, 0, 0)
    const = lambda i, pos, tb, tv, tr: (0, 0)
    grid_spec = pltpu.PrefetchScalarGridSpec(
        num_scalar_prefetch=4,
        grid=(n_tiles,),
        in_specs=[
            pl.BlockSpec(memory_space=pl.ANY),
            pl.BlockSpec((None, D_MODEL, 2 * EXPERT_HIDDEN), e_lo),
            pl.BlockSpec((None, D_MODEL, 2 * EXPERT_HIDDEN), e_hi),
            pl.BlockSpec((None, EXPERT_HIDDEN, D_MODEL), e_lo),
            pl.BlockSpec((None, EXPERT_HIDDEN, D_MODEL), e_hi),
            pl.BlockSpec((1, D_MODEL), const),
            pl.BlockSpec((1, D_MODEL), const),
        ],
        out_specs=pl.BlockSpec(memory_space=pl.ANY),
        scratch_shapes=[
            pltpu.SMEM((n,), jnp.int32),
            pltpu.VMEM((tm * AUG_ROWS, LANES), jnp.float32),
            pltpu.VMEM((tm * AUG_ROWS, LANES), jnp.float32),
            pltpu.VMEM((tm, D_MODEL), jnp.float32),
            pltpu.VMEM((tm, D_MODEL), jnp.float32),
            pltpu.SemaphoreType.DMA((2,)),
            pltpu.SemaphoreType.DMA((2,)),
        ],
    )
    return pl.pallas_call(
        functools.partial(_moe_kernel, n_tokens=n, n_tiles=n_tiles),
        grid_spec=grid_spec,
        out_shape=jax.ShapeDtypeStruct((n, D_MODEL), jnp.float32),
        compiler_params=pltpu.CompilerParams(
            dimension_semantics=("arbitrary",), vmem_limit_bytes=VMEM_LIMIT_BYTES),
        name="moe",
    )(pos, tbk, tval, trow, xs, wgu_bf, wgu_bf, wd_bf, wd_bf, g2, b2)


def _attention_tables(attn_sink):
    slopes = jnp.exp2(-8.0 * jnp.arange(1, N_Q_HEADS + 1, dtype=jnp.float32) / N_Q_HEADS)
    qi = jnp.arange(128)[:, None]
    kj = jnp.arange(3 * 128)[None, :]
    rel = kj - 128 - qi
    dist = jnp.abs(rel).astype(jnp.float32)
    band = jnp.abs(rel) <= WINDOW
    bias = jnp.where(band[None], -slopes[:, None, None] * dist[None], NEG_INF)
    bias = bias.reshape(N_KV_HEADS, 2 * 128, 3 * 128)
    sink = jnp.broadcast_to(attn_sink.astype(jnp.float32)[:, None], (N_Q_HEADS, 128))
    return bias, sink.reshape(N_KV_HEADS, 2 * 128, 1)


def kernel(x, w_in, b_gate, w_dw, b_dw, conv_ln_g, conv_ln_b, w_conv_proj, attn_sink, w_attn_proj,
           w_out, ln1_g, ln1_b, w_router_group, b_router_group, w_router_expert, b_router_expert,
           w_gate_up, w_down, ln2_g, ln2_b):
    nb, s, d = x.shape
    n = nb * s
    assert d == D_MODEL and s % TILE_B == 0 and n % TILE_A == 0 and n % TILE_R == 0
    assert n % TILE_P == 0 and n % TILE_D == 0
    bf = jnp.bfloat16
    l = 0
    row = lambda a: a[l].reshape(1, -1)

    h, q, k, v, gt = _inproj(x.reshape(n, d), w_in[l].astype(bf), row(b_gate))

    wdw8 = jnp.repeat(w_dw[l], SUBLANES, axis=0)
    bias_tab, sink_tab = _attention_tables(attn_sink[l])
    r3 = lambda a: a.reshape(nb, s, a.shape[-1])
    x1 = _mix(r3(h), r3(q), r3(k), r3(v), r3(gt), x, wdw8, row(b_dw), row(conv_ln_g),
              row(conv_ln_b), w_conv_proj[l].astype(bf), bias_tab, sink_tab,
              w_attn_proj[l].astype(bf), w_out[l].astype(bf), row(ln1_g), row(ln1_b))

    pad = SUBLANES - N_GROUPS
    wr = jnp.concatenate([w_router_expert[l].T, w_router_group[l].T,
                          jnp.zeros((pad, d), jnp.float32)], axis=0)
    br = jnp.concatenate([b_router_expert[l], b_router_group[l],
                          jnp.full((pad,), NEG_INF, jnp.float32)]).reshape(-1, 1)
    upper_r = jnp.triu(jnp.ones((TILE_R, TILE_R), bf), k=1)
    x1t, bkt, rank, cnt = _route(x1.reshape(n, d), wr, br, upper_r)

    n_tiles = -(-n // TILE_M) + N_PAIRS
    n_tiles_pad = -(-n_tiles // LANES) * LANES
    upper_b = jnp.triu(jnp.ones((N_BUCKETS, N_BUCKETS), bf), k=1)
    pos, tbk, tval, trow = _plan(bkt, rank, cnt, upper_b, n_tiles_pad)
    pos = pos.reshape(n)

    xs = _dispatch(pos, x1t)
    out = _moe(pos, tbk.reshape(-1), tval.reshape(-1), trow.reshape(-1), xs,
               w_gate_up[l].astype(bf), w_down[l].astype(bf), row(ln2_g), row(ln2_b), n_tiles)
    return out.reshape(nb, s, d)
```

```python
import functools

import jax
import jax.numpy as jnp
from jax import lax
from jax.experimental import pallas as pl
from jax.experimental.pallas import tpu as pltpu

D_MODEL = 1024
N_Q_HEADS = 8
N_KV_HEADS = 4
HEAD_DIM = 128
WINDOW = 128
NEG_INF = -1e30
CONV_WIDTH = 31
CONV_PAD = (CONV_WIDTH - 1) // 2
Q_W = N_Q_HEADS * HEAD_DIM
KV_W = N_KV_HEADS * HEAD_DIM
Q_OFF = 2 * D_MODEL
K_OFF = Q_OFF + Q_W
V_OFF = K_OFF + KV_W
G_OFF = V_OFF + KV_W
IN_WIDTH = G_OFF + 2 * D_MODEL
N_GROUPS = 4
EXPERTS_PER_GROUP = 8
N_EXPERTS = N_GROUPS * EXPERTS_PER_GROUP
EXPERT_HIDDEN = D_MODEL // 4
LN_EPS = 1e-5
DEPTH = 1
DEEPNORM_ALPHA = (2.0 * DEPTH) ** 0.25

LANES = 128
SUBLANES = 8
VMEM_LIMIT_BYTES = 56 * 1024 * 1024

TILE_A = 512
TILE_B = 256
CONV_HALO = 16
CONV_ROWS = 128
ATTN_STREAMS = 8
CONV_STEPS_PER_TURN = 4
TILE_R = 1024
TILE_P = 2048
TILE_D = 2048
TILE_M = 320
SCATTER_GROUP = 32
N_BUCKETS = N_GROUPS * EXPERTS_PER_GROUP * EXPERTS_PER_GROUP
N_PAIRS = N_GROUPS * (EXPERTS_PER_GROUP * (EXPERTS_PER_GROUP - 1) // 2)
ROW_CHUNKS = D_MODEL // LANES
AUG_ROWS = ROW_CHUNKS + 2
_NT = (((1,), (1,)), ((), ()))


def _dot(a, b):
    return jnp.dot(a, b, preferred_element_type=jnp.float32)


def _dot_nt(a, b):
    return lax.dot_general(a, b, _NT, preferred_element_type=jnp.float32)


def _sigmoid(x):
    return 1.0 / (1.0 + jnp.exp(-x))


def _layer_norm(y, g, b):
    mu = jnp.mean(y, axis=-1, keepdims=True)
    d = y - mu
    var = jnp.mean(d * d, axis=-1, keepdims=True)
    return d * lax.rsqrt(var + LN_EPS) * g + b


def _split_bf16(x):
    hi = x.astype(jnp.bfloat16)
    lo = (x - hi.astype(jnp.float32)).astype(jnp.bfloat16)
    return hi, lo


def _inproj_kernel(x_ref, w_ref, bg_ref, h_ref, q_ref, k_ref, v_ref, gt_ref):
    xb = x_ref[...].astype(jnp.bfloat16)
    a = _dot(xb, w_ref[:, 0:D_MODEL])
    g = _dot(xb, w_ref[:, D_MODEL:Q_OFF])
    h_ref[...] = a * _sigmoid(g)
    q = _dot(xb, w_ref[:, Q_OFF:K_OFF]) * (HEAD_DIM ** -0.5)
    q_ref[...] = q.astype(jnp.bfloat16)
    k_ref[...] = _dot(xb, w_ref[:, K_OFF:V_OFF]).astype(jnp.bfloat16)
    v_ref[...] = _dot(xb, w_ref[:, V_OFF:G_OFF]).astype(jnp.bfloat16)
    gt = _dot(xb, w_ref[:, G_OFF:IN_WIDTH]) + bg_ref[...]
    gt_ref[...] = _sigmoid(gt)


def _inproj(x2, w_in_bf, b_gate):
    n = x2.shape[0]
    const = lambda i: (0, 0)
    row = lambda i: (i, 0)
    return pl.pallas_call(
        _inproj_kernel,
        grid=(n // TILE_A,),
        in_specs=[
            pl.BlockSpec((TILE_A, D_MODEL), row),
            pl.BlockSpec((D_MODEL, IN_WIDTH), const, pipeline_mode=pl.Buffered(1)),
            pl.BlockSpec((1, 2 * D_MODEL), const),
        ],
        out_specs=[
            pl.BlockSpec((TILE_A, D_MODEL), row),
            pl.BlockSpec((TILE_A, Q_W), row),
            pl.BlockSpec((TILE_A, KV_W), row),
            pl.BlockSpec((TILE_A, KV_W), row),
            pl.BlockSpec((TILE_A, 2 * D_MODEL), row),
        ],
        out_shape=[
            jax.ShapeDtypeStruct((n, D_MODEL), jnp.float32),
            jax.ShapeDtypeStruct((n, Q_W), jnp.bfloat16),
            jax.ShapeDtypeStruct((n, KV_W), jnp.bfloat16),
            jax.ShapeDtypeStruct((n, KV_W), jnp.bfloat16),
            jax.ShapeDtypeStruct((n, 2 * D_MODEL), jnp.float32),
        ],
        compiler_params=pltpu.CompilerParams(
            dimension_semantics=("arbitrary",), vmem_limit_bytes=VMEM_LIMIT_BYTES),
        name="inproj",
    )(x2, w_in_bf, b_gate)


def _conv_block(hs_ref, wdw_ref, cv_ref, row0, lane0, rows):
    ng = rows // SUBLANES
    nw = ng + 2 * CONV_HALO // SUBLANES
    lanes = slice(lane0, lane0 + LANES)
    win = [hs_ref[row0 + SUBLANES * g: row0 + SUBLANES * (g + 1), lanes] for g in range(nw)]
    sub = lax.broadcasted_iota(jnp.int32, (SUBLANES, LANES), 0)
    acc = [None] * ng
    for s in range(1, SUBLANES + 1):
        if s == SUBLANES:
            shifted = win[1:]
        else:
            rolled = [pltpu.roll(w, SUBLANES - s, axis=0) for w in win]
            keep = sub < (SUBLANES - s)
            shifted = [jnp.where(keep, rolled[g], rolled[g + 1]) for g in range(nw - 1)]
        for a in range((CONV_WIDTH + SUBLANES - 1) // SUBLANES):
            j = s - 1 + SUBLANES * a
            if j >= CONV_WIDTH:
                continue
            wv = wdw_ref[SUBLANES * j: SUBLANES * (j + 1), lanes]
            for g in range(ng):
                term = wv * shifted[g + a]
                acc[g] = term if acc[g] is None else acc[g] + term
        if s == SUBLANES:
            for g, v in enumerate(acc):
                cv_ref[row0 + SUBLANES * g: row0 + SUBLANES * (g + 1), lanes] = v
        yield


def _chain(gens):
    for g in gens:
        yield from g


def _round_robin(streams):
    live = [[g, n] for g, n in streams]
    while live:
        for item in list(live):
            for _ in range(item[1]):
                try:
                    next(item[0])
                except StopIteration:
                    live.remove(item)
                    break


def _mix_kernel(hl_ref, hc_ref, hr_ref, q_ref, kl_ref, kc_ref, kr_ref, vl_ref, vc_ref, vr_ref,
                gt_ref, x_ref, wdw_ref, bdw_ref, cg_ref, cb_ref, wcp_ref, bias_ref, sink_ref,
                wap_ref, wo_ref, g1_ref, b1_ref, x1_ref, hs_ref, cv_ref, at_ref):
    i = pl.program_id(1)
    first = i == 0
    last = i == pl.num_programs(1) - 1
    tb = TILE_B

    hs_ref[0:CONV_HALO, :] = jnp.where(first, 0.0, hl_ref[...])
    hs_ref[CONV_HALO:CONV_HALO + tb, :] = hc_ref[...]
    hs_ref[CONV_HALO + tb:, :] = jnp.where(last, 0.0, hr_ref[...])
    conv_stream = _chain(_conv_block(hs_ref, wdw_ref, cv_ref, r0, lc * LANES, CONV_ROWS)
                         for r0 in range(0, tb, CONV_ROWS) for lc in range(D_MODEL // LANES))

    nq = tb // 128
    neg_l = jnp.where(first, NEG_INF, 0.0)
    neg_r = jnp.where(last, NEG_INF, 0.0)
    lane = lax.broadcasted_iota(jnp.int32, (1, 3 * 128), 1)
    edge_l = jnp.where(lane < 128, neg_l, 0.0)
    edge_r = jnp.where(lane >= 256, neg_r, 0.0)

    def attn_task(j, g):
        rs = slice(j * 128, (j + 1) * 128)
        cs = slice(g * HEAD_DIM, (g + 1) * HEAD_DIM)
        if j == 0:
            k0, v0 = kl_ref[:, cs], vl_ref[:, cs]
        else:
            k0, v0 = kc_ref[(j - 1) * 128:j * 128, cs], vc_ref[(j - 1) * 128:j * 128, cs]
        if j == nq - 1:
            k2, v2 = kr_ref[:, cs], vr_ref[:, cs]
        else:
            k2, v2 = kc_ref[(j + 1) * 128:(j + 2) * 128, cs], vc_ref[(j + 1) * 128:(j + 2) * 128, cs]
        k3 = jnp.concatenate([k0, kc_ref[rs, cs], k2], axis=0)
        v3 = jnp.concatenate([v0, vc_ref[rs, cs], v2], axis=0)
        h0 = 2 * g
        q2 = jnp.concatenate([q_ref[rs, h0 * HEAD_DIM:(h0 + 1) * HEAD_DIM],
                              q_ref[rs, (h0 + 1) * HEAD_DIM:(h0 + 2) * HEAD_DIM]], axis=0)
        s = _dot_nt(q2, k3) + bias_ref[g]
        if j == 0:
            s = s + edge_l
        if j == nq - 1:
            s = s + edge_r
        yield
        sk = sink_ref[g]
        m = jnp.maximum(jnp.max(s, axis=1, keepdims=True), sk)
        p = jnp.exp(s - m)
        yield
        den = jnp.sum(p, axis=1, keepdims=True) + jnp.exp(sk - m)
        o = _dot(p.astype(jnp.bfloat16), v3)
        yield
        o = o / den
        at_ref[rs, h0 * HEAD_DIM:(h0 + 1) * HEAD_DIM] = o[0:128].astype(jnp.bfloat16)
        at_ref[rs, (h0 + 1) * HEAD_DIM:(h0 + 2) * HEAD_DIM] = o[128:256].astype(jnp.bfloat16)
        yield

    attn_tasks = [(j, g) for j in range(nq) for g in range(N_KV_HEADS)]
    streams = []
    for k in range(ATTN_STREAMS):
        streams += [(_chain(attn_task(j, g) for j, g in attn_tasks[k::ATTN_STREAMS]), 1),
                    (conv_stream, CONV_STEPS_PER_TURN)]
    _round_robin(streams)

    c = _layer_norm(cv_ref[...] + bdw_ref[...], cg_ref[...], cb_ref[...])
    c = (c * _sigmoid(c)).astype(jnp.bfloat16)
    conv_out = _dot(c, wcp_ref[...])
    attn_out = _dot(at_ref[...], wap_ref[...])

    merged = gt_ref[:, 0:D_MODEL] * conv_out + gt_ref[:, D_MODEL:] * attn_out
    mixed = _dot(merged.astype(jnp.bfloat16), wo_ref[...])
    x1_ref[...] = _layer_norm(DEEPNORM_ALPHA * x_ref[...] + mixed, g1_ref[...], b1_ref[...])


def _mix(h3, q3, k3, v3, gt3, x3, wdw8, b_dw, cg, cb, wcp, bias_tab, sink_tab, wap, wo, g1, b1):
    nb, s, _ = x3.shape
    tb = TILE_B
    nt = s // tb
    hb = tb // CONV_HALO
    kb = tb // 128
    const2 = lambda b, i: (0, 0)
    const3 = lambda b, i: (0, 0, 0)
    cur = lambda b, i: (b, i, 0)
    w_spec = lambda shape: pl.BlockSpec(shape, const2, pipeline_mode=pl.Buffered(1))
    in_specs = [
        pl.BlockSpec((None, CONV_HALO, D_MODEL), lambda b, i: (b, jnp.maximum(i * hb - 1, 0), 0)),
        pl.BlockSpec((None, tb, D_MODEL), cur),
        pl.BlockSpec((None, CONV_HALO, D_MODEL),
                     lambda b, i: (b, jnp.minimum((i + 1) * hb, s // CONV_HALO - 1), 0)),
        pl.BlockSpec((None, tb, Q_W), cur),
    ]
    for _ in range(2):
        in_specs += [
            pl.BlockSpec((None, 128, KV_W), lambda b, i: (b, jnp.maximum(i * kb - 1, 0), 0)),
            pl.BlockSpec((None, tb, KV_W), cur),
            pl.BlockSpec((None, 128, KV_W), lambda b, i: (b, jnp.minimum((i + 1) * kb, s // 128 - 1), 0)),
        ]
    in_specs += [
        pl.BlockSpec((None, tb, 2 * D_MODEL), cur),
        pl.BlockSpec((None, tb, D_MODEL), cur),
        w_spec((CONV_WIDTH * SUBLANES, D_MODEL)),
        pl.BlockSpec((1, D_MODEL), const2),
        pl.BlockSpec((1, D_MODEL), const2),
        pl.BlockSpec((1, D_MODEL), const2),
        w_spec((D_MODEL, D_MODEL)),
        pl.BlockSpec((N_KV_HEADS, 256, 3 * 128), const3, pipeline_mode=pl.Buffered(1)),
        pl.BlockSpec((N_KV_HEADS, 256, 1), const3),
        w_spec((Q_W, D_MODEL)),
        w_spec((D_MODEL, D_MODEL)),
        pl.BlockSpec((1, D_MODEL), const2),
        pl.BlockSpec((1, D_MODEL), const2),
    ]
    return pl.pallas_call(
        _mix_kernel,
        grid=(nb, nt),
        in_specs=in_specs,
        out_specs=pl.BlockSpec((None, tb, D_MODEL), cur),
        out_shape=jax.ShapeDtypeStruct((nb, s, D_MODEL), jnp.float32),
        scratch_shapes=[
            pltpu.VMEM((tb + 2 * CONV_HALO, D_MODEL), jnp.float32),
            pltpu.VMEM((tb, D_MODEL), jnp.float32),
            pltpu.VMEM((tb, Q_W), jnp.bfloat16),
        ],
        compiler_params=pltpu.CompilerParams(
            dimension_semantics=("arbitrary", "arbitrary"), vmem_limit_bytes=VMEM_LIMIT_BYTES),
        name="mix",
    )(h3, h3, h3, q3, k3, k3, k3, v3, v3, v3, gt3, x3, wdw8, b_dw, cg, cb, wcp, bias_tab,
      sink_tab, wap, wo, g1, b1)


def _route_kernel(x_ref, wr_ref, br_ref, u_ref, aug_ref, bkt_ref, rank_ref, cnt_ref, carry_ref):
    i = pl.program_id(0)
    tr = TILE_R

    @pl.when(i == 0)
    def _():
        carry_ref[...] = jnp.zeros_like(carry_ref)

    x = x_ref[...]
    xh, xl = _split_bf16(x)
    wh, wl = _split_bf16(wr_ref[...])
    lt = _dot_nt(wh, xh) + _dot_nt(wh, xl) + _dot_nt(wl, xh) + br_ref[...]
    le = lt[0:N_EXPERTS]
    lg = lt[N_EXPERTS:N_EXPERTS + SUBLANES]
    iota8 = lax.broadcasted_iota(jnp.int32, (SUBLANES, tr), 0)

    gmax = jnp.max(lg, axis=0, keepdims=True)
    gidx = jnp.min(jnp.where(lg == gmax, iota8, SUBLANES), axis=0, keepdims=True)
    gw = 1.0 / jnp.sum(jnp.exp(lg - gmax), axis=0, keepdims=True)

    ein = le[0:EXPERTS_PER_GROUP]
    for g in range(1, N_GROUPS):
        ein = jnp.where(gidx == g, le[g * EXPERTS_PER_GROUP:(g + 1) * EXPERTS_PER_GROUP], ein)
    ee = jnp.exp(ein - jnp.max(ein, axis=0, keepdims=True))
    prob = ee / jnp.sum(ee, axis=0, keepdims=True)
    p1 = jnp.max(prob, axis=0, keepdims=True)
    i1 = jnp.min(jnp.where(prob == p1, iota8, SUBLANES), axis=0, keepdims=True)
    rest = jnp.where(iota8 == i1, -1.0, prob)
    p2 = jnp.max(rest, axis=0, keepdims=True)
    i2 = jnp.min(jnp.where(rest == p2, iota8, SUBLANES), axis=0, keepdims=True)
    den = p1 + p2
    w1 = gw * (p1 / den)
    w2 = gw * (p2 / den)
    first_lo = i1 < i2
    lo = jnp.minimum(i1, i2)
    hi = jnp.maximum(i1, i2)
    w_lo = jnp.where(first_lo, w1, w2)
    w_hi = jnp.where(first_lo, w2, w1)
    bkt = gidx * (EXPERTS_PER_GROUP * EXPERTS_PER_GROUP) + lo * EXPERTS_PER_GROUP + hi
    bkt_ref[...] = bkt

    onehot = lax.broadcasted_iota(jnp.int32, (N_BUCKETS, tr), 0) == bkt
    ob = jnp.where(onehot, 1.0, 0.0).astype(jnp.bfloat16)
    before = _dot(ob, u_ref[...])
    within = jnp.sum(jnp.where(onehot, before, 0.0), axis=0, keepdims=True)
    carry = carry_ref[...]
    c_hi = jnp.floor(carry * (1.0 / 256.0))
    c_lo = carry - 256.0 * c_hi
    prev = 256.0 * _dot(c_hi.astype(jnp.bfloat16), ob) + _dot(c_lo.astype(jnp.bfloat16), ob)
    rank_ref[...] = (within + prev[0:1]).astype(jnp.int32)
    carry_ref[...] = carry + _dot_nt(jnp.ones((SUBLANES, tr), jnp.bfloat16), ob)

    @pl.when(i == pl.num_programs(0) - 1)
    def _():
        cnt_ref[...] = carry_ref[...]

    for c in range(ROW_CHUNKS):
        aug_ref[pl.ds(c, tr, stride=AUG_ROWS), :] = x[:, c * LANES:(c + 1) * LANES]
    aug_ref[pl.ds(ROW_CHUNKS, tr, stride=AUG_ROWS), :] = jnp.broadcast_to(w_lo, (LANES, tr)).T
    aug_ref[pl.ds(ROW_CHUNKS + 1, tr, stride=AUG_ROWS), :] = jnp.broadcast_to(w_hi, (LANES, tr)).T


def _route(x1, wr, br, upper):
    n = x1.shape[0]
    tr = TILE_R
    const = lambda i: (0, 0)
    return pl.pallas_call(
        _route_kernel,
        grid=(n // tr,),
        in_specs=[
            pl.BlockSpec((tr, D_MODEL), lambda i: (i, 0)),
            pl.BlockSpec(wr.shape, const),
            pl.BlockSpec(br.shape, const),
            pl.BlockSpec((tr, tr), const),
        ],
        out_specs=[
            pl.BlockSpec((tr * AUG_ROWS, LANES), lambda i: (i, 0)),
            pl.BlockSpec((1, tr), lambda i: (0, i)),
            pl.BlockSpec((1, tr), lambda i: (0, i)),
            pl.BlockSpec((SUBLANES, N_BUCKETS), const),
        ],
        out_shape=[
            jax.ShapeDtypeStruct((n * AUG_ROWS, LANES), jnp.float32),
            jax.ShapeDtypeStruct((1, n), jnp.int32),
            jax.ShapeDtypeStruct((1, n), jnp.int32),
            jax.ShapeDtypeStruct((SUBLANES, N_BUCKETS), jnp.float32),
        ],
        scratch_shapes=[pltpu.VMEM((SUBLANES, N_BUCKETS), jnp.float32)],
        compiler_params=pltpu.CompilerParams(
            dimension_semantics=("arbitrary",), vmem_limit_bytes=VMEM_LIMIT_BYTES),
        name="route",
    )(x1, wr, br, upper)


def _plan_kernel(bkt_ref, rank_ref, cnt_ref, u_ref, pos_ref, tb_ref, tv_ref, tr_ref, *,
                 n_tiles_pad):
    tm = TILE_M
    cnt = cnt_ref[...]
    ntl = jnp.floor((cnt + (tm - 0.5)) / tm)
    upper = u_ref[...]
    tstart = _dot(ntl.astype(jnp.bfloat16), upper)
    tend = tstart + ntl
    c_hi = jnp.floor(cnt * (1.0 / 256.0))
    c_lo = cnt - 256.0 * c_hi
    bstart = (256.0 * _dot(c_hi.astype(jnp.bfloat16), upper)
              + _dot(c_lo.astype(jnp.bfloat16), upper))

    def pick(table, onehot_bf):
        t_hi = jnp.floor(table * (1.0 / 256.0))
        t_lo = table - 256.0 * t_hi
        return (256.0 * _dot(t_hi.astype(jnp.bfloat16), onehot_bf)
                + _dot(t_lo.astype(jnp.bfloat16), onehot_bf))[0:1]

    bkt = bkt_ref[...]
    onehot = lax.broadcasted_iota(jnp.int32, (N_BUCKETS, bkt.shape[1]), 0) == bkt
    ob = jnp.where(onehot, 1.0, 0.0).astype(jnp.bfloat16)
    pos_ref[...] = pick(bstart, ob).astype(jnp.int32) + rank_ref[...]

    eye = (lax.broadcasted_iota(jnp.int32, (N_BUCKETS, N_BUCKETS), 0)
           == lax.broadcasted_iota(jnp.int32, (N_BUCKETS, N_BUCKETS), 1))
    tend_col = jnp.sum(jnp.where(eye, tend[0:1], 0.0), axis=1, keepdims=True)
    tile = lax.broadcasted_iota(jnp.int32, (1, n_tiles_pad), 1).astype(jnp.float32)
    tbk = jnp.sum(jnp.where(tend_col <= tile, 1.0, 0.0), axis=0, keepdims=True)
    tbk_i = tbk.astype(jnp.int32)
    oh_t = lax.broadcasted_iota(jnp.int32, (N_BUCKETS, n_tiles_pad), 0) == tbk_i
    oh_tb = jnp.where(oh_t, 1.0, 0.0).astype(jnp.bfloat16)
    done = (tile - pick(tstart, oh_tb)) * tm
    valid = jnp.clip(pick(cnt, oh_tb) - done, 0.0, float(tm))
    tv_ref[...] = valid.astype(jnp.int32)
    tr_ref[...] = jnp.where(valid > 0.0, pick(bstart, oh_tb) + done, 0.0).astype(jnp.int32)
    tb_ref[...] = jnp.minimum(tbk_i, N_BUCKETS - 1)


def _plan(bkt, rank, cnt, upper, n_tiles_pad):
    n = bkt.shape[1]
    const = lambda i: (0, 0)
    tile_spec = pl.BlockSpec((1, n_tiles_pad), const)
    tile_shape = jax.ShapeDtypeStruct((1, n_tiles_pad), jnp.int32)
    return pl.pallas_call(
        functools.partial(_plan_kernel, n_tiles_pad=n_tiles_pad),
        grid=(n // TILE_P,),
        in_specs=[
            pl.BlockSpec((1, TILE_P), lambda i: (0, i)),
            pl.BlockSpec((1, TILE_P), lambda i: (0, i)),
            pl.BlockSpec((SUBLANES, N_BUCKETS), const),
            pl.BlockSpec((N_BUCKETS, N_BUCKETS), const),
        ],
        out_specs=[pl.BlockSpec((1, TILE_P), lambda i: (0, i)), tile_spec, tile_spec, tile_spec],
        out_shape=[jax.ShapeDtypeStruct((1, n), jnp.int32), tile_shape, tile_shape, tile_shape],
        compiler_params=pltpu.CompilerParams(dimension_semantics=("arbitrary",)),
        name="plan",
    )(bkt, rank, cnt, upper)


def _dispatch_kernel(pos_ref, x_ref, xs_hbm, zbuf, sem, zsem, *, n_tokens):
    i = pl.program_id(0)
    td = TILE_D

    @pl.when(i == 0)
    def _():
        zbuf[...] = jnp.zeros_like(zbuf)
        tail = pltpu.make_async_copy(
            zbuf, xs_hbm.at[pl.ds(n_tokens * AUG_ROWS, TILE_M * AUG_ROWS), :], zsem.at[0])
        tail.start()
        tail.wait()

    copies = [pltpu.make_async_copy(
        x_ref.at[pl.ds(r * AUG_ROWS, AUG_ROWS), :],
        xs_hbm.at[pl.ds(pos_ref[i * td + r] * AUG_ROWS, AUG_ROWS), :],
        sem.at[0]) for r in range(td)]
    for r, cp in enumerate(copies):
        cp.start(priority=r % 2)
    for cp in copies:
        cp.wait()


def _dispatch(pos, x1t):
    n = pos.shape[0]
    td = TILE_D
    grid_spec = pltpu.PrefetchScalarGridSpec(
        num_scalar_prefetch=1,
        grid=(n // td,),
        in_specs=[pl.BlockSpec((td * AUG_ROWS, LANES), lambda i, pos: (i, 0))],
        out_specs=pl.BlockSpec(memory_space=pl.ANY),
        scratch_shapes=[
            pltpu.VMEM((TILE_M * AUG_ROWS, LANES), jnp.float32),
            pltpu.SemaphoreType.DMA((1,)),
            pltpu.SemaphoreType.DMA((1,)),
        ],
    )
    return pl.pallas_call(
        functools.partial(_dispatch_kernel, n_tokens=n),
        grid_spec=grid_spec,
        out_shape=jax.ShapeDtypeStruct(((n + TILE_M) * AUG_ROWS, LANES), jnp.float32),
        compiler_params=pltpu.CompilerParams(
            dimension_semantics=("arbitrary",), vmem_limit_bytes=VMEM_LIMIT_BYTES),
        name="dispatch",
    )(pos, x1t)


def _moe_kernel(pos_ref, tb_ref, tv_ref, tr_ref, xs_hbm, wgu1_ref, wgu2_ref, wd1_ref, wd2_ref,
                g2_ref, b2_ref, out_hbm, src_ref, xbuf0, xbuf1, obuf0, obuf1, gsem, ssem, *,
                n_tokens, n_tiles):
    i = pl.program_id(0)
    tm = TILE_M
    xbufs = (xbuf0, xbuf1)
    obufs = (obuf0, obuf1)

    def tile_copy(slot, tile):
        return pltpu.make_async_copy(
            xs_hbm.at[pl.ds(tr_ref[tile] * AUG_ROWS, tm * AUG_ROWS), :], xbufs[slot], gsem.at[slot])

    def scatter_copy(slot, dst, r):
        return pltpu.make_async_copy(
            obufs[slot].at[pl.ds(r, 1), :], out_hbm.at[pl.ds(dst, 1), :], ssem.at[slot])

    def for_valid_rows(tile, fn):
        nv = tv_ref[tile]
        for g in range(tm // SCATTER_GROUP):
            @pl.when(nv >= (g + 1) * SCATTER_GROUP)
            def _(g=g):
                for k in range(SCATTER_GROUP):
                    fn(g * SCATTER_GROUP + k, k)
        p = SCATTER_GROUP // 2
        while p:
            @pl.when(jnp.bitwise_and(nv, p) != 0)
            def _(p=p):
                base = jnp.bitwise_and(nv, -2 * p)
                for k in range(p):
                    fn(base + k, k)
            p //= 2

    def start_scatter(slot, tile):
        row0 = tr_ref[tile]
        for_valid_rows(tile, lambda r, k: scatter_copy(slot, src_ref[row0 + r], r)
                       .start(priority=k % 2))

    def wait_scatter(slot, tile):
        for_valid_rows(tile, lambda r, k: scatter_copy(slot, 0, 0).wait())

    @pl.when(i == 0)
    def _():
        def invert(c, _):
            for u in range(8):
                t = c * 8 + u
                src_ref[pos_ref[t]] = t
            return 0
        lax.fori_loop(0, n_tokens // 8, invert, 0)
        tile_copy(0, 0).start()

    def tile_body(slot):
        has_next = jnp.logical_and(i + 1 < n_tiles, tv_ref[jnp.minimum(i + 1, n_tiles - 1)] > 0)
        tile_copy(slot, i).wait()
        tile_copy(1 - slot, jnp.where(has_next, i + 1, i)).start()

        xb_ref = xbufs[slot]
        chunk = lambda c: xb_ref[pl.ds(c, tm, stride=AUG_ROWS), :]
        x = jnp.concatenate([chunk(c) for c in range(ROW_CHUNKS)], axis=1)
        xb = x.astype(jnp.bfloat16)
        outs = [None, None]

        def expert(e, wgu_ref, wd_ref):
            gu = _dot(xb, wgu_ref[...])
            yield
            hg = gu[:, 0:EXPERT_HIDDEN]
            hu = gu[:, EXPERT_HIDDEN:]
            hid = (hg * _sigmoid(hg) * hu).astype(jnp.bfloat16)
            yield
            y = _dot(hid, wd_ref[...])
            yield
            outs[e] = jnp.concatenate([chunk(ROW_CHUNKS + e)] * ROW_CHUNKS, axis=1) * y
            yield

        _round_robin([(expert(0, wgu1_ref, wd1_ref), 1), (expert(1, wgu2_ref, wd2_ref), 1)])
        res = _layer_norm(DEEPNORM_ALPHA * x + (outs[0] + outs[1]), g2_ref[...], b2_ref[...])

        @pl.when(i >= 2)
        def _():
            wait_scatter(slot, i - 2)

        obufs[slot][...] = res
        start_scatter(slot, i)

        @pl.when(jnp.logical_not(has_next))
        def _():
            tile_copy(1 - slot, i).wait()

            @pl.when(i >= 1)
            def _():
                wait_scatter(1 - slot, i - 1)
            wait_scatter(slot, i)

    used = tv_ref[i] > 0
    for slot in (0, 1):
        @pl.when(jnp.logical_and(used, i % 2 == slot))
        def _(slot=slot):
            tile_body(slot)


def _moe(pos, tbk, tval, trow, xs, wgu_bf, wd_bf, g2, b2, n_tiles):
    n = pos.shape[0]
    tm = TILE_M
    epg = EXPERTS_PER_GROUP
    e_lo = lambda i, pos, tb, tv, tr: (tb[i] // epg, 0, 0)
    e_hi = lambda i, pos, tb, tv, tr: ((tb[i] // (epg * epg)) * epg + tb[i] % epg, 0, 0)
    const = lambda i, pos, tb, tv, tr: (0, 0)
    grid_spec = pltpu.PrefetchScalarGridSpec(
        num_scalar_prefetch=4,
        grid=(n_tiles,),
        in_specs=[
            pl.BlockSpec(memory_space=pl.ANY),
            pl.BlockSpec((None, D_MODEL, 2 * EXPERT_HIDDEN), e_lo),
            pl.BlockSpec((None, D_MODEL, 2 * EXPERT_HIDDEN), e_hi),
            pl.BlockSpec((None, EXPERT_HIDDEN, D_MODEL), e_lo),
            pl.BlockSpec((None, EXPERT_HIDDEN, D_MODEL), e_hi),
            pl.BlockSpec((1, D_MODEL), const),
            pl.BlockSpec((1, D_MODEL), const),
        ],
        out_specs=pl.BlockSpec(memory_space=pl.ANY),
        scratch_shapes=[
            pltpu.SMEM((n,), jnp.int32),
            pltpu.VMEM((tm * AUG_ROWS, LANES), jnp.float32),
            pltpu.VMEM((tm * AUG_ROWS, LANES), jnp.float32),
            pltpu.VMEM((tm, D_MODEL), jnp.float32),
            pltpu.VMEM((tm, D_MODEL), jnp.float32),
            pltpu.SemaphoreType.DMA((2,)),
            pltpu.SemaphoreType.DMA((2,)),
        ],
    )
    return pl.pallas_call(
        functools.partial(_moe_kernel, n_tokens=n, n_tiles=n_tiles),
        grid_spec=grid_spec,
        out_shape=jax.ShapeDtypeStruct((n, D_MODEL), jnp.float32),
        compiler_params=pltpu.CompilerParams(
            dimension_semantics=("arbitrary",), vmem_limit_bytes=VMEM_LIMIT_BYTES),
        name="moe",
    )(pos, tbk, tval, trow, xs, wgu_bf, wgu_bf, wd_bf, wd_bf, g2, b2)


def _attention_tables(attn_sink):
    slopes = jnp.exp2(-8.0 * jnp.arange(1, N_Q_HEADS + 1, dtype=jnp.float32) / N_Q_HEADS)
    qi = jnp.arange(128)[:, None]
    kj = jnp.arange(3 * 128)[None, :]
    rel = kj - 128 - qi
    dist = jnp.abs(rel).astype(jnp.float32)
    band = jnp.abs(rel) <= WINDOW
    bias = jnp.where(band[None], -slopes[:, None, None] * dist[None], NEG_INF)
    bias = bias.reshape(N_KV_HEADS, 2 * 128, 3 * 128)
    sink = jnp.broadcast_to(attn_sink.astype(jnp.float32)[:, None], (N_Q_HEADS, 128))
    return bias, sink.reshape(N_KV_HEADS, 2 * 128, 1)


def kernel(x, w_in, b_gate, w_dw, b_dw, conv_ln_g, conv_ln_b, w_conv_proj, attn_sink, w_attn_proj,
           w_out, ln1_g, ln1_b, w_router_group, b_router_group, w_router_expert, b_router_expert,
           w_gate_up, w_down, ln2_g, ln2_b):
    nb, s, d = x.shape
    n = nb * s
    assert d == D_MODEL and s % TILE_B == 0 and n % TILE_A == 0 and n % TILE_R == 0
    assert n % TILE_P == 0 and n % TILE_D == 0
    bf = jnp.bfloat16
    l = 0
    row = lambda a: a[l].reshape(1, -1)

    h, q, k, v, gt = _inproj(x.reshape(n, d), w_in[l].astype(bf), row(b_gate))

    wdw8 = jnp.repeat(w_dw[l], SUBLANES, axis=0)
    bias_tab, sink_tab = _attention_tables(attn_sink[l])
    r3 = lambda a: a.reshape(nb, s, a.shape[-1])
    x1 = _mix(r3(h), r3(q), r3(k), r3(v), r3(gt), x, wdw8, row(b_dw), row(conv_ln_g),
              row(conv_ln_b), w_conv_proj[l].astype(bf), bias_tab, sink_tab,
              w_attn_proj[l].astype(bf), w_out[l].astype(bf), row(ln1_g), row(ln1_b))

    pad = SUBLANES - N_GROUPS
    wr = jnp.concatenate([w_router_expert[l].T, w_router_group[l].T,
                          jnp.zeros((pad, d), jnp.float32)], axis=0)
    br = jnp.concatenate([b_router_expert[l], b_router_group[l],
                          jnp.full((pad,), NEG_INF, jnp.float32)]).reshape(-1, 1)
    upper_r = jnp.triu(jnp.ones((TILE_R, TILE_R), bf), k=1)
    x1t, bkt, rank, cnt = _route(x1.reshape(n, d), wr, br, upper_r)

    n_tiles = -(-n // TILE_M) + N_PAIRS
    n_tiles_pad = -(-n_tiles // LANES) * LANES
    upper_b = jnp.triu(jnp.ones((N_BUCKETS, N_BUCKETS), bf), k=1)
    pos, tbk, tval, trow = _plan(bkt, rank, cnt, upper_b, n_tiles_pad)
    pos = pos.reshape(n)

    xs = _dispatch(pos, x1t)
    out = _moe(pos, tbk.reshape(-1), tval.reshape(-1), trow.reshape(-1), xs,
               w_gate_up[l].astype(bf), w_down[l].astype(bf), row(ln2_g), row(ln2_b), n_tiles)
    return out.reshape(nb, s, d)
```

```python
import functools

import jax
import jax.numpy as jnp
from jax import lax
from jax.experimental import pallas as pl
from jax.experimental.pallas import tpu as pltpu

D_MODEL = 1024
N_Q_HEADS = 8
N_KV_HEADS = 4
HEAD_DIM = 128
WINDOW = 128
NEG_INF = -1e30
CONV_WIDTH = 31
CONV_PAD = (CONV_WIDTH - 1) // 2
Q_W = N_Q_HEADS * HEAD_DIM
KV_W = N_KV_HEADS * HEAD_DIM
Q_OFF = 2 * D_MODEL
K_OFF = Q_OFF + Q_W
V_OFF = K_OFF + KV_W
G_OFF = V_OFF + KV_W
IN_WIDTH = G_OFF + 2 * D_MODEL
N_GROUPS = 4
EXPERTS_PER_GROUP = 8
N_EXPERTS = N_GROUPS * EXPERTS_PER_GROUP
EXPERT_HIDDEN = D_MODEL // 4
LN_EPS = 1e-5
DEPTH = 1
DEEPNORM_ALPHA = (2.0 * DEPTH) ** 0.25
LOG2_E = 1.4426950408889634

LANES = 128
SUBLANES = 8
VMEM_LIMIT_BYTES = 56 * 1024 * 1024

TILE_A = 512
TILE_B = 256
CONV_HALO = 16
CONV_ROWS = 128
ATTN_STREAMS = 8
CONV_STEPS_PER_TURN = 4
TILE_R = 1024
TILE_P = 2048
TILE_D = 2048
TILE_M = 320
SCATTER_GROUP = 32
N_BUCKETS = N_GROUPS * EXPERTS_PER_GROUP * EXPERTS_PER_GROUP
N_PAIRS = N_GROUPS * (EXPERTS_PER_GROUP * (EXPERTS_PER_GROUP - 1) // 2)
ROW_CHUNKS = D_MODEL // LANES
AUG_ROWS = ROW_CHUNKS + 2
_NT = (((1,), (1,)), ((), ()))


def _dot(a, b):
    return jnp.dot(a, b, preferred_element_type=jnp.float32)


def _dot_nt(a, b):
    return lax.dot_general(a, b, _NT, preferred_element_type=jnp.float32)


def _sigmoid(x):
    return 1.0 / (1.0 + jnp.exp(-x))


def _layer_norm(y, g, b):
    mu = jnp.mean(y, axis=-1, keepdims=True)
    d = y - mu
    var = jnp.mean(d * d, axis=-1, keepdims=True)
    return d * lax.rsqrt(var + LN_EPS) * g + b


def _split_bf16(x):
    hi = x.astype(jnp.bfloat16)
    lo = (x - hi.astype(jnp.float32)).astype(jnp.bfloat16)
    return hi, lo


def _inproj_kernel(x_ref, w_ref, bg_ref, h_ref, q_ref, k_ref, v_ref, gt_ref):
    xb = x_ref[...].astype(jnp.bfloat16)
    a = _dot(xb, w_ref[:, 0:D_MODEL])
    g = _dot(xb, w_ref[:, D_MODEL:Q_OFF])
    h_ref[...] = a * _sigmoid(g)
    q = _dot(xb, w_ref[:, Q_OFF:K_OFF]) * (HEAD_DIM ** -0.5 * LOG2_E)
    q_ref[...] = q.astype(jnp.bfloat16)
    k_ref[...] = _dot(xb, w_ref[:, K_OFF:V_OFF]).astype(jnp.bfloat16)
    v_ref[...] = _dot(xb, w_ref[:, V_OFF:G_OFF]).astype(jnp.bfloat16)
    gt = _dot(xb, w_ref[:, G_OFF:IN_WIDTH]) + bg_ref[...]
    gt_ref[...] = _sigmoid(gt)


def _inproj(x2, w_in_bf, b_gate):
    n = x2.shape[0]
    const = lambda i: (0, 0)
    row = lambda i: (i, 0)
    return pl.pallas_call(
        _inproj_kernel,
        grid=(n // TILE_A,),
        in_specs=[
            pl.BlockSpec((TILE_A, D_MODEL), row),
            pl.BlockSpec((D_MODEL, IN_WIDTH), const, pipeline_mode=pl.Buffered(1)),
            pl.BlockSpec((1, 2 * D_MODEL), const),
        ],
        out_specs=[
            pl.BlockSpec((TILE_A, D_MODEL), row),
            pl.BlockSpec((TILE_A, Q_W), row),
            pl.BlockSpec((TILE_A, KV_W), row),
            pl.BlockSpec((TILE_A, KV_W), row),
            pl.BlockSpec((TILE_A, 2 * D_MODEL), row),
        ],
        out_shape=[
            jax.ShapeDtypeStruct((n, D_MODEL), jnp.float32),
            jax.ShapeDtypeStruct((n, Q_W), jnp.bfloat16),
            jax.ShapeDtypeStruct((n, KV_W), jnp.bfloat16),
            jax.ShapeDtypeStruct((n, KV_W), jnp.bfloat16),
            jax.ShapeDtypeStruct((n, 2 * D_MODEL), jnp.float32),
        ],
        compiler_params=pltpu.CompilerParams(
            dimension_semantics=("arbitrary",), vmem_limit_bytes=VMEM_LIMIT_BYTES),
        name="inproj",
    )(x2, w_in_bf, b_gate)


def _conv_block(hs_ref, wdw_ref, cv_ref, row0, lane0, rows):
    ng = rows // SUBLANES
    nw = ng + 2 * CONV_HALO // SUBLANES
    lanes = slice(lane0, lane0 + LANES)
    win = [hs_ref[row0 + SUBLANES * g: row0 + SUBLANES * (g + 1), lanes] for g in range(nw)]
    sub = lax.broadcasted_iota(jnp.int32, (SUBLANES, LANES), 0)
    acc = [None] * ng
    for s in range(1, SUBLANES + 1):
        if s == SUBLANES:
            shifted = win[1:]
        else:
            rolled = [pltpu.roll(w, SUBLANES - s, axis=0) for w in win]
            keep = sub < (SUBLANES - s)
            shifted = [jnp.where(keep, rolled[g], rolled[g + 1]) for g in range(nw - 1)]
        for a in range((CONV_WIDTH + SUBLANES - 1) // SUBLANES):
            j = s - 1 + SUBLANES * a
            if j >= CONV_WIDTH:
                continue
            wv = wdw_ref[SUBLANES * j: SUBLANES * (j + 1), lanes]
            for g in range(ng):
                term = wv * shifted[g + a]
                acc[g] = term if acc[g] is None else acc[g] + term
        if s == SUBLANES:
            for g, v in enumerate(acc):
                cv_ref[row0 + SUBLANES * g: row0 + SUBLANES * (g + 1), lanes] = v
        yield


def _chain(gens):
    for g in gens:
        yield from g


def _round_robin(streams):
    live = [[g, n] for g, n in streams]
    while live:
        for item in list(live):
            for _ in range(item[1]):
                try:
                    next(item[0])
                except StopIteration:
                    live.remove(item)
                    break


def _mix_kernel(hl_ref, hc_ref, hr_ref, q_ref, kl_ref, kc_ref, kr_ref, vl_ref, vc_ref, vr_ref,
                gt_ref, x_ref, wdw_ref, bdw_ref, cg_ref, cb_ref, wcp_ref, bias_ref, sink_ref,
                wap_ref, wo_ref, g1_ref, b1_ref, x1_ref, hs_ref, cv_ref, at_ref):
    i = pl.program_id(1)
    first = i == 0
    last = i == pl.num_programs(1) - 1
    tb = TILE_B

    hs_ref[0:CONV_HALO, :] = jnp.where(first, 0.0, hl_ref[...])
    hs_ref[CONV_HALO:CONV_HALO + tb, :] = hc_ref[...]
    hs_ref[CONV_HALO + tb:, :] = jnp.where(last, 0.0, hr_ref[...])
    conv_stream = _chain(_conv_block(hs_ref, wdw_ref, cv_ref, r0, lc * LANES, CONV_ROWS)
                         for r0 in range(0, tb, CONV_ROWS) for lc in range(D_MODEL // LANES))

    nq = tb // 128
    neg_l = jnp.where(first, NEG_INF, 0.0)
    neg_r = jnp.where(last, NEG_INF, 0.0)
    lane = lax.broadcasted_iota(jnp.int32, (1, 3 * 128), 1)
    edge_l = jnp.where(lane < 128, neg_l, 0.0)
    edge_r = jnp.where(lane >= 256, neg_r, 0.0)

    def attn_task(j, g):
        rs = slice(j * 128, (j + 1) * 128)
        cs = slice(g * HEAD_DIM, (g + 1) * HEAD_DIM)
        if j == 0:
            k0, v0 = kl_ref[:, cs], vl_ref[:, cs]
        else:
            k0, v0 = kc_ref[(j - 1) * 128:j * 128, cs], vc_ref[(j - 1) * 128:j * 128, cs]
        if j == nq - 1:
            k2, v2 = kr_ref[:, cs], vr_ref[:, cs]
        else:
            k2, v2 = kc_ref[(j + 1) * 128:(j + 2) * 128, cs], vc_ref[(j + 1) * 128:(j + 2) * 128, cs]
        k3 = jnp.concatenate([k0, kc_ref[rs, cs], k2], axis=0)
        v3 = jnp.concatenate([v0, vc_ref[rs, cs], v2], axis=0)
        h0 = 2 * g
        q2 = jnp.concatenate([q_ref[rs, h0 * HEAD_DIM:(h0 + 1) * HEAD_DIM],
                              q_ref[rs, (h0 + 1) * HEAD_DIM:(h0 + 2) * HEAD_DIM]], axis=0)
        s = _dot_nt(q2, k3) + bias_ref[g]
        if j == 0:
            s = s + edge_l
        if j == nq - 1:
            s = s + edge_r
        yield
        sk = sink_ref[g]
        m = jnp.maximum(jnp.max(s, axis=1, keepdims=True), sk)
        p = jnp.exp2(s - m)
        yield
        den = jnp.sum(p, axis=1, keepdims=True) + jnp.exp2(sk - m)
        o = _dot(p.astype(jnp.bfloat16), v3)
        yield
        o = o / den
        at_ref[rs, h0 * HEAD_DIM:(h0 + 1) * HEAD_DIM] = o[0:128].astype(jnp.bfloat16)
        at_ref[rs, (h0 + 1) * HEAD_DIM:(h0 + 2) * HEAD_DIM] = o[128:256].astype(jnp.bfloat16)
        yield

    attn_tasks = [(j, g) for j in range(nq) for g in range(N_KV_HEADS)]
    streams = []
    for k in range(ATTN_STREAMS):
        streams += [(_chain(attn_task(j, g) for j, g in attn_tasks[k::ATTN_STREAMS]), 1),
                    (conv_stream, CONV_STEPS_PER_TURN)]
    _round_robin(streams)

    c = _layer_norm(cv_ref[...] + bdw_ref[...], cg_ref[...], cb_ref[...])
    c = (c * _sigmoid(c)).astype(jnp.bfloat16)
    conv_out = _dot(c, wcp_ref[...])
    attn_out = _dot(at_ref[...], wap_ref[...])

    merged = gt_ref[:, 0:D_MODEL] * conv_out + gt_ref[:, D_MODEL:] * attn_out
    mixed = _dot(merged.astype(jnp.bfloat16), wo_ref[...])
    x1_ref[...] = _layer_norm(DEEPNORM_ALPHA * x_ref[...] + mixed, g1_ref[...], b1_ref[...])


def _mix(h3, q3, k3, v3, gt3, x3, wdw8, b_dw, cg, cb, wcp, bias_tab, sink_tab, wap, wo, g1, b1):
    nb, s, _ = x3.shape
    tb = TILE_B
    nt = s // tb
    hb = tb // CONV_HALO
    kb = tb // 128
    const2 = lambda b, i: (0, 0)
    const3 = lambda b, i: (0, 0, 0)
    cur = lambda b, i: (b, i, 0)
    w_spec = lambda shape: pl.BlockSpec(shape, const2, pipeline_mode=pl.Buffered(1))
    in_specs = [
        pl.BlockSpec((None, CONV_HALO, D_MODEL), lambda b, i: (b, jnp.maximum(i * hb - 1, 0), 0)),
        pl.BlockSpec((None, tb, D_MODEL), cur),
        pl.BlockSpec((None, CONV_HALO, D_MODEL),
                     lambda b, i: (b, jnp.minimum((i + 1) * hb, s // CONV_HALO - 1), 0)),
        pl.BlockSpec((None, tb, Q_W), cur),
    ]
    for _ in range(2):
        in_specs += [
            pl.BlockSpec((None, 128, KV_W), lambda b, i: (b, jnp.maximum(i * kb - 1, 0), 0)),
            pl.BlockSpec((None, tb, KV_W), cur),
            pl.BlockSpec((None, 128, KV_W), lambda b, i: (b, jnp.minimum((i + 1) * kb, s // 128 - 1), 0)),
        ]
    in_specs += [
        pl.BlockSpec((None, tb, 2 * D_MODEL), cur),
        pl.BlockSpec((None, tb, D_MODEL), cur),
        w_spec((CONV_WIDTH * SUBLANES, D_MODEL)),
        pl.BlockSpec((1, D_MODEL), const2),
        pl.BlockSpec((1, D_MODEL), const2),
        pl.BlockSpec((1, D_MODEL), const2),
        w_spec((D_MODEL, D_MODEL)),
        pl.BlockSpec((N_KV_HEADS, 256, 3 * 128), const3, pipeline_mode=pl.Buffered(1)),
        pl.BlockSpec((N_KV_HEADS, 256, 1), const3),
        w_spec((Q_W, D_MODEL)),
        w_spec((D_MODEL, D_MODEL)),
        pl.BlockSpec((1, D_MODEL), const2),
        pl.BlockSpec((1, D_MODEL), const2),
    ]
    return pl.pallas_call(
        _mix_kernel,
        grid=(nb, nt),
        in_specs=in_specs,
        out_specs=pl.BlockSpec((None, tb, D_MODEL), cur),
        out_shape=jax.ShapeDtypeStruct((nb, s, D_MODEL), jnp.float32),
        scratch_shapes=[
            pltpu.VMEM((tb + 2 * CONV_HALO, D_MODEL), jnp.float32),
            pltpu.VMEM((tb, D_MODEL), jnp.float32),
            pltpu.VMEM((tb, Q_W), jnp.bfloat16),
        ],
        compiler_params=pltpu.CompilerParams(
            dimension_semantics=("arbitrary", "arbitrary"), vmem_limit_bytes=VMEM_LIMIT_BYTES),
        name="mix",
    )(h3, h3, h3, q3, k3, k3, k3, v3, v3, v3, gt3, x3, wdw8, b_dw, cg, cb, wcp, bias_tab,
      sink_tab, wap, wo, g1, b1)


def _route_kernel(x_ref, wr_ref, br_ref, u_ref, aug_ref, bkt_ref, rank_ref, cnt_ref, carry_ref):
    i = pl.program_id(0)
    tr = TILE_R

    @pl.when(i == 0)
    def _():
        carry_ref[...] = jnp.zeros_like(carry_ref)

    x = x_ref[...]
    xh, xl = _split_bf16(x)
    wh, wl = _split_bf16(wr_ref[...])
    lt = _dot_nt(wh, xh) + _dot_nt(wh, xl) + _dot_nt(wl, xh) + br_ref[...]
    le = lt[0:N_EXPERTS]
    lg = lt[N_EXPERTS:N_EXPERTS + SUBLANES]
    iota8 = lax.broadcasted_iota(jnp.int32, (SUBLANES, tr), 0)

    gmax = jnp.max(lg, axis=0, keepdims=True)
    gidx = jnp.min(jnp.where(lg == gmax, iota8, SUBLANES), axis=0, keepdims=True)
    gw = 1.0 / jnp.sum(jnp.exp(lg - gmax), axis=0, keepdims=True)

    ein = le[0:EXPERTS_PER_GROUP]
    for g in range(1, N_GROUPS):
        ein = jnp.where(gidx == g, le[g * EXPERTS_PER_GROUP:(g + 1) * EXPERTS_PER_GROUP], ein)
    ee = jnp.exp(ein - jnp.max(ein, axis=0, keepdims=True))
    prob = ee / jnp.sum(ee, axis=0, keepdims=True)
    p1 = jnp.max(prob, axis=0, keepdims=True)
    i1 = jnp.min(jnp.where(prob == p1, iota8, SUBLANES), axis=0, keepdims=True)
    rest = jnp.where(iota8 == i1, -1.0, prob)
    p2 = jnp.max(rest, axis=0, keepdims=True)
    i2 = jnp.min(jnp.where(rest == p2, iota8, SUBLANES), axis=0, keepdims=True)
    den = p1 + p2
    w1 = gw * (p1 / den)
    w2 = gw * (p2 / den)
    first_lo = i1 < i2
    lo = jnp.minimum(i1, i2)
    hi = jnp.maximum(i1, i2)
    w_lo = jnp.where(first_lo, w1, w2)
    w_hi = jnp.where(first_lo, w2, w1)
    bkt = gidx * (EXPERTS_PER_GROUP * EXPERTS_PER_GROUP) + lo * EXPERTS_PER_GROUP + hi
    bkt_ref[...] = bkt

    onehot = lax.broadcasted_iota(jnp.int32, (N_BUCKETS, tr), 0) == bkt
    ob = jnp.where(onehot, 1.0, 0.0).astype(jnp.bfloat16)
    before = _dot(ob, u_ref[...])
    within = jnp.sum(jnp.where(onehot, before, 0.0), axis=0, keepdims=True)
    carry = carry_ref[...]
    c_hi = jnp.floor(carry * (1.0 / 256.0))
    c_lo = carry - 256.0 * c_hi
    prev = 256.0 * _dot(c_hi.astype(jnp.bfloat16), ob) + _dot(c_lo.astype(jnp.bfloat16), ob)
    rank_ref[...] = (within + prev[0:1]).astype(jnp.int32)
    carry_ref[...] = carry + _dot_nt(jnp.ones((SUBLANES, tr), jnp.bfloat16), ob)

    @pl.when(i == pl.num_programs(0) - 1)
    def _():
        cnt_ref[...] = carry_ref[...]

    for c in range(ROW_CHUNKS):
        aug_ref[pl.ds(c, tr, stride=AUG_ROWS), :] = x[:, c * LANES:(c + 1) * LANES]
    aug_ref[pl.ds(ROW_CHUNKS, tr, stride=AUG_ROWS), :] = jnp.broadcast_to(w_lo, (LANES, tr)).T
    aug_ref[pl.ds(ROW_CHUNKS + 1, tr, stride=AUG_ROWS), :] = jnp.broadcast_to(w_hi, (LANES, tr)).T


def _route(x1, wr, br, upper):
    n = x1.shape[0]
    tr = TILE_R
    const = lambda i: (0, 0)
    return pl.pallas_call(
        _route_kernel,
        grid=(n // tr,),
        in_specs=[
            pl.BlockSpec((tr, D_MODEL), lambda i: (i, 0)),
            pl.BlockSpec(wr.shape, const),
            pl.BlockSpec(br.shape, const),
            pl.BlockSpec((tr, tr), const),
        ],
        out_specs=[
            pl.BlockSpec((tr * AUG_ROWS, LANES), lambda i: (i, 0)),
            pl.BlockSpec((1, tr), lambda i: (0, i)),
            pl.BlockSpec((1, tr), lambda i: (0, i)),
            pl.BlockSpec((SUBLANES, N_BUCKETS), const),
        ],
        out_shape=[
            jax.ShapeDtypeStruct((n * AUG_ROWS, LANES), jnp.float32),
            jax.ShapeDtypeStruct((1, n), jnp.int32),
            jax.ShapeDtypeStruct((1, n), jnp.int32),
            jax.ShapeDtypeStruct((SUBLANES, N_BUCKETS), jnp.float32),
        ],
        scratch_shapes=[pltpu.VMEM((SUBLANES, N_BUCKETS), jnp.float32)],
        compiler_params=pltpu.CompilerParams(
            dimension_semantics=("arbitrary",), vmem_limit_bytes=VMEM_LIMIT_BYTES),
        name="route",
    )(x1, wr, br, upper)


def _plan_kernel(bkt_ref, rank_ref, cnt_ref, u_ref, pos_ref, tb_ref, tv_ref, tr_ref, *,
                 n_tiles_pad):
    tm = TILE_M
    cnt = cnt_ref[...]
    ntl = jnp.floor((cnt + (tm - 0.5)) / tm)
    upper = u_ref[...]
    tstart = _dot(ntl.astype(jnp.bfloat16), upper)
    tend = tstart + ntl
    c_hi = jnp.floor(cnt * (1.0 / 256.0))
    c_lo = cnt - 256.0 * c_hi
    bstart = (256.0 * _dot(c_hi.astype(jnp.bfloat16), upper)
              + _dot(c_lo.astype(jnp.bfloat16), upper))

    def pick(table, onehot_bf):
        t_hi = jnp.floor(table * (1.0 / 256.0))
        t_lo = table - 256.0 * t_hi
        return (256.0 * _dot(t_hi.astype(jnp.bfloat16), onehot_bf)
                + _dot(t_lo.astype(jnp.bfloat16), onehot_bf))[0:1]

    bkt = bkt_ref[...]
    onehot = lax.broadcasted_iota(jnp.int32, (N_BUCKETS, bkt.shape[1]), 0) == bkt
    ob = jnp.where(onehot, 1.0, 0.0).astype(jnp.bfloat16)
    pos_ref[...] = pick(bstart, ob).astype(jnp.int32) + rank_ref[...]

    eye = (lax.broadcasted_iota(jnp.int32, (N_BUCKETS, N_BUCKETS), 0)
           == lax.broadcasted_iota(jnp.int32, (N_BUCKETS, N_BUCKETS), 1))
    tend_col = jnp.sum(jnp.where(eye, tend[0:1], 0.0), axis=1, keepdims=True)
    tile = lax.broadcasted_iota(jnp.int32, (1, n_tiles_pad), 1).astype(jnp.float32)
    tbk = jnp.sum(jnp.where(tend_col <= tile, 1.0, 0.0), axis=0, keepdims=True)
    tbk_i = tbk.astype(jnp.int32)
    oh_t = lax.broadcasted_iota(jnp.int32, (N_BUCKETS, n_tiles_pad), 0) == tbk_i
    oh_tb = jnp.where(oh_t, 1.0, 0.0).astype(jnp.bfloat16)
    done = (tile - pick(tstart, oh_tb)) * tm
    valid = jnp.clip(pick(cnt, oh_tb) - done, 0.0, float(tm))
    tv_ref[...] = valid.astype(jnp.int32)
    tr_ref[...] = jnp.where(valid > 0.0, pick(bstart, oh_tb) + done, 0.0).astype(jnp.int32)
    tb_ref[...] = jnp.minimum(tbk_i, N_BUCKETS - 1)


def _plan(bkt, rank, cnt, upper, n_tiles_pad):
    n = bkt.shape[1]
    const = lambda i: (0, 0)
    tile_spec = pl.BlockSpec((1, n_tiles_pad), const)
    tile_shape = jax.ShapeDtypeStruct((1, n_tiles_pad), jnp.int32)
    return pl.pallas_call(
        functools.partial(_plan_kernel, n_tiles_pad=n_tiles_pad),
        grid=(n // TILE_P,),
        in_specs=[
            pl.BlockSpec((1, TILE_P), lambda i: (0, i)),
            pl.BlockSpec((1, TILE_P), lambda i: (0, i)),
            pl.BlockSpec((SUBLANES, N_BUCKETS), const),
            pl.BlockSpec((N_BUCKETS, N_BUCKETS), const),
        ],
        out_specs=[pl.BlockSpec((1, TILE_P), lambda i: (0, i)), tile_spec, tile_spec, tile_spec],
        out_shape=[jax.ShapeDtypeStruct((1, n), jnp.int32), tile_shape, tile_shape, tile_shape],
        compiler_params=pltpu.CompilerParams(dimension_semantics=("arbitrary",)),
        name="plan",
    )(bkt, rank, cnt, upper)


def _dispatch_kernel(pos_ref, x_ref, xs_hbm, zbuf, sem, zsem, *, n_tokens):
    i = pl.program_id(0)
    td = TILE_D

    @pl.when(i == 0)
    def _():
        zbuf[...] = jnp.zeros_like(zbuf)
        tail = pltpu.make_async_copy(
            zbuf, xs_hbm.at[pl.ds(n_tokens * AUG_ROWS, TILE_M * AUG_ROWS), :], zsem.at[0])
        tail.start()
        tail.wait()

    copies = [pltpu.make_async_copy(
        x_ref.at[pl.ds(r * AUG_ROWS, AUG_ROWS), :],
        xs_hbm.at[pl.ds(pos_ref[i * td + r] * AUG_ROWS, AUG_ROWS), :],
        sem.at[0]) for r in range(td)]
    for r, cp in enumerate(copies):
        cp.start(priority=r % 2)
    for cp in copies:
        cp.wait()


def _dispatch(pos, x1t):
    n = pos.shape[0]
    td = TILE_D
    grid_spec = pltpu.PrefetchScalarGridSpec(
        num_scalar_prefetch=1,
        grid=(n // td,),
        in_specs=[pl.BlockSpec((td * AUG_ROWS, LANES), lambda i, pos: (i, 0))],
        out_specs=pl.BlockSpec(memory_space=pl.ANY),
        scratch_shapes=[
            pltpu.VMEM((TILE_M * AUG_ROWS, LANES), jnp.float32),
            pltpu.SemaphoreType.DMA((1,)),
            pltpu.SemaphoreType.DMA((1,)),
        ],
    )
    return pl.pallas_call(
        functools.partial(_dispatch_kernel, n_tokens=n),
        grid_spec=grid_spec,
        out_shape=jax.ShapeDtypeStruct(((n + TILE_M) * AUG_ROWS, LANES), jnp.float32),
        compiler_params=pltpu.CompilerParams(
            dimension_semantics=("arbitrary",), vmem_limit_bytes=VMEM_LIMIT_BYTES),
        name="dispatch",
    )(pos, x1t)


def _moe_kernel(pos_ref, tb_ref, tv_ref, tr_ref, xs_hbm, wgu1_ref, wgu2_ref, wd1_ref, wd2_ref,
                g2_ref, b2_ref, out_hbm, src_ref, xbuf0, xbuf1, obuf0, obuf1, gsem, ssem, *,
                n_tokens, n_tiles):
    i = pl.program_id(0)
    tm = TILE_M
    xbufs = (xbuf0, xbuf1)
    obufs = (obuf0, obuf1)

    def tile_copy(slot, tile):
        return pltpu.make_async_copy(
            xs_hbm.at[pl.ds(tr_ref[tile] * AUG_ROWS, tm * AUG_ROWS), :], xbufs[slot], gsem.at[slot])

    def scatter_copy(slot, dst, r):
        return pltpu.make_async_copy(
            obufs[slot].at[pl.ds(r, 1), :], out_hbm.at[pl.ds(dst, 1), :], ssem.at[slot])

    def for_valid_rows(tile, fn):
        nv = tv_ref[tile]
        for g in range(tm // SCATTER_GROUP):
            @pl.when(nv >= (g + 1) * SCATTER_GROUP)
            def _(g=g):
                for k in range(SCATTER_GROUP):
                    fn(g * SCATTER_GROUP + k, k)
        p = SCATTER_GROUP // 2
        while p:
            @pl.when(jnp.bitwise_and(nv, p) != 0)
            def _(p=p):
                base = jnp.bitwise_and(nv, -2 * p)
                for k in range(p):
                    fn(base + k, k)
            p //= 2

    def start_scatter(slot, tile):
        row0 = tr_ref[tile]
        for_valid_rows(tile, lambda r, k: scatter_copy(slot, src_ref[row0 + r], r)
                       .start(priority=k % 2))

    def wait_scatter(slot, tile):
        for_valid_rows(tile, lambda r, k: scatter_copy(slot, 0, 0).wait())

    @pl.when(i == 0)
    def _():
        def invert(c, _):
            for u in range(8):
                t = c * 8 + u
                src_ref[pos_ref[t]] = t
            return 0
        lax.fori_loop(0, n_tokens // 8, invert, 0)
        tile_copy(0, 0).start()

    def tile_body(slot):
        has_next = jnp.logical_and(i + 1 < n_tiles, tv_ref[jnp.minimum(i + 1, n_tiles - 1)] > 0)
        tile_copy(slot, i).wait()
        tile_copy(1 - slot, jnp.where(has_next, i + 1, i)).start()

        xb_ref = xbufs[slot]
        chunk = lambda c: xb_ref[pl.ds(c, tm, stride=AUG_ROWS), :]
        x = jnp.concatenate([chunk(c) for c in range(ROW_CHUNKS)], axis=1)
        xb = x.astype(jnp.bfloat16)
        outs = [None, None]

        def expert(e, wgu_ref, wd_ref):
            gu = _dot(xb, wgu_ref[...])
            yield
            hg = gu[:, 0:EXPERT_HIDDEN]
            hu = gu[:, EXPERT_HIDDEN:]
            hid = (hg * _sigmoid(hg) * hu).astype(jnp.bfloat16)
            yield
            y = _dot(hid, wd_ref[...])
            yield
            outs[e] = jnp.concatenate([chunk(ROW_CHUNKS + e)] * ROW_CHUNKS, axis=1) * y
            yield

        _round_robin([(expert(0, wgu1_ref, wd1_ref), 1), (expert(1, wgu2_ref, wd2_ref), 1)])
        res = _layer_norm(DEEPNORM_ALPHA * x + (outs[0] + outs[1]), g2_ref[...], b2_ref[...])

        @pl.when(i >= 2)
        def _():
            wait_scatter(slot, i - 2)

        obufs[slot][...] = res
        start_scatter(slot, i)

        @pl.when(jnp.logical_not(has_next))
        def _():
            tile_copy(1 - slot, i).wait()

            @pl.when(i >= 1)
            def _():
                wait_scatter(1 - slot, i - 1)
            wait_scatter(slot, i)

    used = tv_ref[i] > 0
    for slot in (0, 1):
        @pl.when(jnp.logical_and(used, i % 2 == slot))
        def _(slot=slot):
            tile_body(slot)


def _moe(pos, tbk, tval, trow, xs, wgu_bf, wd_bf, g2, b2, n_tiles):
    n = pos.shape[0]
    tm = TILE_M
    epg = EXPERTS_PER_GROUP
    e_lo = lambda i, pos, tb, tv, tr: (tb[i] // epg, 0, 0)
    e_hi = lambda i, pos, tb, tv, tr: ((tb[i] // (epg * epg)) * epg + tb[i] % epg, 0, 0)
    const = lambda i, pos, tb, tv, tr: (0, 0)
    grid_spec = pltpu.PrefetchScalarGridSpec(
        num_scalar_prefetch=4,
        grid=(n_tiles,),
        in_specs=[
            pl.BlockSpec(memory_space=pl.ANY),
            pl.BlockSpec((None, D_MODEL, 2 * EXPERT_HIDDEN), e_lo),
            pl.BlockSpec((None, D_MODEL, 2 * EXPERT_HIDDEN), e_hi),
            pl.BlockSpec((None, EXPERT_HIDDEN, D_MODEL), e_lo),
            pl.BlockSpec((None, EXPERT_HIDDEN, D_MODEL), e_hi),
            pl.BlockSpec((1, D_MODEL), const),
            pl.BlockSpec((1, D_MODEL), const),
        ],
        out_specs=pl.BlockSpec(memory_space=pl.ANY),
        scratch_shapes=[
            pltpu.SMEM((n,), jnp.int32),
            pltpu.VMEM((tm * AUG_ROWS, LANES), jnp.float32),
            pltpu.VMEM((tm * AUG_ROWS, LANES), jnp.float32),
            pltpu.VMEM((tm, D_MODEL), jnp.float32),
            pltpu.VMEM((tm, D_MODEL), jnp.float32),
            pltpu.SemaphoreType.DMA((2,)),
            pltpu.SemaphoreType.DMA((2,)),
        ],
    )
    return pl.pallas_call(
        functools.partial(_moe_kernel, n_tokens=n, n_tiles=n_tiles),
        grid_spec=grid_spec,
        out_shape=jax.ShapeDtypeStruct((n, D_MODEL), jnp.float32),
        compiler_params=pltpu.CompilerParams(
            dimension_semantics=("arbitrary",), vmem_limit_bytes=VMEM_LIMIT_BYTES),
        name="moe",
    )(pos, tbk, tval, trow, xs, wgu_bf, wgu_bf, wd_bf, wd_bf, g2, b2)


def _attention_tables(attn_sink):
    slopes = jnp.exp2(-8.0 * jnp.arange(1, N_Q_HEADS + 1, dtype=jnp.float32) / N_Q_HEADS)
    qi = jnp.arange(128)[:, None]
    kj = jnp.arange(3 * 128)[None, :]
    rel = kj - 128 - qi
    dist = jnp.abs(rel).astype(jnp.float32)
    band = jnp.abs(rel) <= WINDOW
    bias = jnp.where(band[None], -slopes[:, None, None] * dist[None] * LOG2_E, NEG_INF)
    bias = bias.reshape(N_KV_HEADS, 2 * 128, 3 * 128)
    sink = jnp.broadcast_to(attn_sink.astype(jnp.float32)[:, None] * LOG2_E, (N_Q_HEADS, 128))
    return bias, sink.reshape(N_KV_HEADS, 2 * 128, 1)


def kernel(x, w_in, b_gate, w_dw, b_dw, conv_ln_g, conv_ln_b, w_conv_proj, attn_sink, w_attn_proj,
           w_out, ln1_g, ln1_b, w_router_group, b_router_group, w_router_expert, b_router_expert,
           w_gate_up, w_down, ln2_g, ln2_b):
    nb, s, d = x.shape
    n = nb * s
    assert d == D_MODEL and s % TILE_B == 0 and n % TILE_A == 0 and n % TILE_R == 0
    assert n % TILE_P == 0 and n % TILE_D == 0
    bf = jnp.bfloat16
    l = 0
    row = lambda a: a[l].reshape(1, -1)

    h, q, k, v, gt = _inproj(x.reshape(n, d), w_in[l].astype(bf), row(b_gate))

    wdw8 = jnp.repeat(w_dw[l], SUBLANES, axis=0)
    bias_tab, sink_tab = _attention_tables(attn_sink[l])
    r3 = lambda a: a.reshape(nb, s, a.shape[-1])
    x1 = _mix(r3(h), r3(q), r3(k), r3(v), r3(gt), x, wdw8, row(b_dw), row(conv_ln_g),
              row(conv_ln_b), w_conv_proj[l].astype(bf), bias_tab, sink_tab,
              w_attn_proj[l].astype(bf), w_out[l].astype(bf), row(ln1_g), row(ln1_b))

    pad = SUBLANES - N_GROUPS
    wr = jnp.concatenate([w_router_expert[l].T, w_router_group[l].T,
                          jnp.zeros((pad, d), jnp.float32)], axis=0)
    br = jnp.concatenate([b_router_expert[l], b_router_group[l],
                          jnp.full((pad,), NEG_INF, jnp.float32)]).reshape(-1, 1)
    upper_r = jnp.triu(jnp.ones((TILE_R, TILE_R), bf), k=1)
    x1t, bkt, rank, cnt = _route(x1.reshape(n, d), wr, br, upper_r)

    n_tiles = -(-n // TILE_M) + N_PAIRS
    n_tiles_pad = -(-n_tiles // LANES) * LANES
    upper_b = jnp.triu(jnp.ones((N_BUCKETS, N_BUCKETS), bf), k=1)
    pos, tbk, tval, trow = _plan(bkt, rank, cnt, upper_b, n_tiles_pad)
    pos = pos.reshape(n)

    xs = _dispatch(pos, x1t)
    out = _moe(pos, tbk.reshape(-1), tval.reshape(-1), trow.reshape(-1), xs,
               w_gate_up[l].astype(bf), w_down[l].astype(bf), row(ln2_g), row(ln2_b), n_tiles)
    return out.reshape(nb, s, d)
```
